```python
import jax, jax.numpy as jnp
from jax import lax
import numpy as np

D_MODEL = 4096
BATCH = 4
SEQ = 4096
DEPTH = 2

N_MIXERS = 2
RG_HEADS = 16
D_RNN = ((4 * D_MODEL // 3 + 127) // 128) * 128
RG_HEAD_DIM = D_RNN // RG_HEADS
RG_CONV = 4
RG_C = 8.0
HG_KEY_DIM = 128
HG_HEADS = D_MODEL // HG_KEY_DIM
HG_VAL_DIM = D_MODEL // HG_HEADS
HG_CHUNK = 32
N_EXPERTS = 16
N_GROUPS = 4
EXPERTS_PER_GROUP = N_EXPERTS // N_GROUPS
TOP_K = 2
EXPERT_FF = 3 * D_MODEL // 8
N_RG_LAYERS = (DEPTH + 1) // 2
N_HG_LAYERS = DEPTH // 2
DEEPNORM_ALPHA = (2.0 * DEPTH) ** 0.25
DEEPNORM_BETA = (8.0 * DEPTH) ** -0.25
LN_EPS = 1e-5
RMS_EPS = 1e-6

kernel_name = "hybrid_rglru_hgrn2_grouped_moe_deepnorm"


def layer_norm(x, g, b):
    xf = x.astype(jnp.float32)
    mu = jnp.mean(xf, axis=-1, keepdims=True)
    xc = xf - mu
    var = jnp.mean(xc * xc, axis=-1, keepdims=True)
    return (xc * lax.rsqrt(var + LN_EPS) * g.astype(jnp.float32) + b.astype(jnp.float32)).astype(x.dtype)


def causal_depthwise_conv(x, w, b):
    c = x.shape[-1]
    y = lax.conv_general_dilated(
        x, w[:, None, :].astype(x.dtype), window_strides=(1,), padding=[(RG_CONV - 1, 0)],
        dimension_numbers=("NWC", "WIO", "NWC"), feature_group_count=c)
    return y + b.astype(x.dtype)


def linear_recurrence(a, u):
    def step(hc, au):
        a_t, u_t = au
        hc = a_t * hc + u_t
        return hc, hc
    h0 = jnp.zeros(a.shape[:1] + a.shape[2:], jnp.float32)
    _, hs = lax.scan(step, h0, (jnp.swapaxes(a, 0, 1), jnp.swapaxes(u, 0, 1)))
    return jnp.swapaxes(hs, 0, 1)


def rglru_mixer(h, w_in, conv_w, conv_b, w_ra, b_ra, w_ix, b_ix, lam, w_out):
    bsz, seq, _ = h.shape
    gate_in, xr = jnp.split(h @ w_in, 2, axis=-1)
    xc = causal_depthwise_conv(xr, conv_w, conv_b)
    xh = xc.reshape(bsz, seq, RG_HEADS, RG_HEAD_DIM)
    r = jax.nn.sigmoid((jnp.einsum("bshi,hij->bshj", xh, w_ra) + b_ra).astype(jnp.float32))
    ig = jax.nn.sigmoid((jnp.einsum("bshi,hij->bshj", xh, w_ix) + b_ix).astype(jnp.float32))
    log_a = -RG_C * jax.nn.softplus(-lam.astype(jnp.float32)) * r
    a = jnp.exp(log_a)
    u = jnp.sqrt(-jnp.expm1(2.0 * log_a)) * (ig * xh.astype(jnp.float32))
    hs = linear_recurrence(a, u).astype(h.dtype)
    y = hs.reshape(bsz, seq, D_RNN) * jax.nn.gelu(gate_in)
    return y @ w_out


def chunked_gated_recurrence(q, k, v, log_f):
    bsz, seq, nh, dk = q.shape
    dv = v.shape[-1]
    n_chunks = seq // HG_CHUNK

    def to_chunks(t):
        return t.reshape(bsz, n_chunks, HG_CHUNK, nh, -1).transpose(1, 0, 3, 2, 4)

    causal = jnp.tril(jnp.ones((HG_CHUNK, HG_CHUNK), dtype=bool))[None, None, :, :, None]

    def step(state, inp):
        qc, kc, vc, gc = inp
        b = jnp.cumsum(gc, axis=2)
        diff = b[:, :, :, None, :] - b[:, :, None, :, :]
        decay = jnp.exp(jnp.where(causal, diff, -jnp.inf))
        scores = jnp.einsum("bhtk,bhsk,bhtsk->bhts", qc, kc, decay)
        o = (jnp.einsum("bhts,bhsv->bhtv", scores, vc)
             + jnp.einsum("bhtk,bhkv->bhtv", qc * jnp.exp(b), state))
        b_last = b[:, :, -1:, :]
        state = (state * jnp.exp(b_last)[:, :, 0, :, None]
                 + jnp.einsum("bhsk,bhsv->bhkv", kc * jnp.exp(b_last - b), vc))
        return state, o

    s0 = jnp.zeros((bsz, nh, dk, dv), jnp.float32)
    _, o = lax.scan(step, s0, (to_chunks(q), to_chunks(k), to_chunks(v), to_chunks(log_f)))
    return o.transpose(1, 0, 3, 2, 4).reshape(bsz, seq, nh, dv)


def hgrn2_mixer(h, w_in, norm_w, w_out, lb_logits, layer):
    bsz, seq, _ = h.shape
    hk = HG_HEADS * HG_KEY_DIM
    hv = HG_HEADS * HG_VAL_DIM
    q, z, v, g = jnp.split(h @ w_in, [hk, 2 * hk, 2 * hk + hv], axis=-1)
    probs = jax.nn.softmax(lb_logits.astype(jnp.float32), axis=0)
    lb = jnp.sum(probs[1:layer + 1], axis=0).reshape(HG_HEADS, HG_KEY_DIM)
    z = z.astype(jnp.float32).reshape(bsz, seq, HG_HEADS, HG_KEY_DIM)
    log_f = jnp.logaddexp(jnp.log(lb), jnp.log1p(-lb) + jax.nn.log_sigmoid(z))
    k = (1.0 - lb) * jax.nn.sigmoid(-z)
    qf = jax.nn.silu(q.astype(jnp.float32)).reshape(bsz, seq, HG_HEADS, HG_KEY_DIM)
    vf = v.astype(jnp.float32).reshape(bsz, seq, HG_HEADS, HG_VAL_DIM)
    o = chunked_gated_recurrence(qf, k, vf, log_f)
    o = o * lax.rsqrt(jnp.mean(o * o, axis=-1, keepdims=True) + RMS_EPS) * norm_w.astype(jnp.float32)
    o = o.reshape(bsz, seq, hv).astype(h.dtype) * jax.nn.silu(g)
    return o @ w_out


def grouped_moe(h, router_w, router_b, w_in, w_out):
    bsz, seq, d = h.shape
    hf = h.reshape(-1, d)
    logits = hf.astype(jnp.float32) @ router_w.astype(jnp.float32) + router_b.astype(jnp.float32)
    probs = jax.nn.softmax(logits, axis=-1)
    grouped = probs.reshape(-1, N_GROUPS, EXPERTS_PER_GROUP)
    group_score = jnp.sum(lax.top_k(grouped, TOP_K)[0], axis=-1)
    g_sel = jnp.argmax(group_score, axis=-1)
    in_group = jnp.take_along_axis(grouped, g_sel[:, None, None], axis=1)[:, 0]
    top_p, top_i = lax.top_k(in_group, TOP_K)
    expert_ids = g_sel[:, None] * EXPERTS_PER_GROUP + top_i
    weights = top_p / jnp.sum(top_p, axis=-1, keepdims=True)
    combine = jnp.einsum("nk,nke->ne", weights,
                         jax.nn.one_hot(expert_ids, N_EXPERTS, dtype=jnp.float32)).astype(h.dtype)
    y = jnp.zeros_like(hf)
    for e in range(N_EXPERTS):
        gate, up = jnp.split(hf @ w_in[e], 2, axis=-1)
        y = y + combine[:, e:e + 1] * ((jax.nn.silu(gate) * up) @ w_out[e])
    return y.reshape(bsz, seq, d)


def setup_inputs(seed: int = 0) -> dict:
    key = jax.random.key(seed)
    ks = jax.random.split(key, 20)
    f32 = jnp.float32

    def nrm(k, shape, scale):
        return jax.random.normal(k, shape, f32) * scale

    x = nrm(ks[0], (BATCH, SEQ, D_MODEL), 1.0)
    rg_w_in = nrm(ks[1], (N_RG_LAYERS, D_MODEL, 2 * D_RNN), D_MODEL ** -0.5)
    rg_conv_w = nrm(ks[2], (N_RG_LAYERS, RG_CONV, D_RNN), RG_CONV ** -0.5)
    rg_conv_b = nrm(ks[3], (N_RG_LAYERS, D_RNN), 0.01)
    rg_w_ra = nrm(ks[4], (N_RG_LAYERS, RG_HEADS, RG_HEAD_DIM, RG_HEAD_DIM), RG_HEAD_DIM ** -0.5)
    rg_b_ra = nrm(ks[5], (N_RG_LAYERS, RG_HEADS, RG_HEAD_DIM), 0.01)
    rg_w_ix = nrm(ks[6], (N_RG_LAYERS, RG_HEADS, RG_HEAD_DIM, RG_HEAD_DIM), RG_HEAD_DIM ** -0.5)
    rg_b_ix = nrm(ks[7], (N_RG_LAYERS, RG_HEADS, RG_HEAD_DIM), 0.01)
    u = jax.random.uniform(ks[8], (N_RG_LAYERS, RG_HEADS, RG_HEAD_DIM), f32, 0.9, 0.999)
    p = u ** (1.0 / RG_C)
    rg_lambda = jnp.log(p) - jnp.log1p(-p)
    rg_w_out = nrm(ks[9], (N_RG_LAYERS, D_RNN, D_MODEL), D_RNN ** -0.5 * DEEPNORM_BETA)
    hg_w_in = nrm(ks[10], (N_HG_LAYERS, D_MODEL, 2 * HG_HEADS * HG_KEY_DIM + 2 * HG_HEADS * HG_VAL_DIM),
                  D_MODEL ** -0.5)
    hg_norm_w = 1.0 + nrm(ks[11], (N_HG_LAYERS, HG_VAL_DIM), 0.01)
    hg_w_out = nrm(ks[12], (N_HG_LAYERS, HG_HEADS * HG_VAL_DIM, D_MODEL),
                   (HG_HEADS * HG_VAL_DIM) ** -0.5 * DEEPNORM_BETA)
    hg_lb_logits = nrm(ks[13], (DEPTH, HG_HEADS * HG_KEY_DIM), 0.1)
    router_w = nrm(ks[14], (D_MODEL, N_EXPERTS), D_MODEL ** -0.5)
    router_b = nrm(ks[15], (N_EXPERTS,), 0.01)
    moe_w_in = nrm(ks[16], (DEPTH, N_EXPERTS, D_MODEL, 2 * EXPERT_FF), D_MODEL ** -0.5)
    moe_w_out = nrm(ks[17], (DEPTH, N_EXPERTS, EXPERT_FF, D_MODEL), EXPERT_FF ** -0.5 * DEEPNORM_BETA)
    ln_g = 1.0 + nrm(ks[18], (DEPTH, 2, D_MODEL), 0.01)
    ln_b = nrm(ks[19], (DEPTH, 2, D_MODEL), 0.01)
    return {"x": x, "rg_w_in": rg_w_in, "rg_conv_w": rg_conv_w, "rg_conv_b": rg_conv_b,
            "rg_w_ra": rg_w_ra, "rg_b_ra": rg_b_ra, "rg_w_ix": rg_w_ix, "rg_b_ix": rg_b_ix,
            "rg_lambda": rg_lambda, "rg_w_out": rg_w_out, "hg_w_in": hg_w_in, "hg_norm_w": hg_norm_w,
            "hg_w_out": hg_w_out, "hg_lb_logits": hg_lb_logits, "router_w": router_w,
            "router_b": router_b, "moe_w_in": moe_w_in, "moe_w_out": moe_w_out,
            "ln_g": ln_g, "ln_b": ln_b}


def reference(x, rg_w_in, rg_conv_w, rg_conv_b, rg_w_ra, rg_b_ra, rg_w_ix, rg_b_ix, rg_lambda,
              rg_w_out, hg_w_in, hg_norm_w, hg_w_out, hg_lb_logits, router_w, router_b,
              moe_w_in, moe_w_out, ln_g, ln_b):
    h = x
    for layer in range(DEPTH):
        j = layer // N_MIXERS
        if layer % N_MIXERS == 0:
            mix = rglru_mixer(h, rg_w_in[j], rg_conv_w[j], rg_conv_b[j], rg_w_ra[j], rg_b_ra[j],
                              rg_w_ix[j], rg_b_ix[j], rg_lambda[j], rg_w_out[j])
        else:
            mix = hgrn2_mixer(h, hg_w_in[j], hg_norm_w[j], hg_w_out[j], hg_lb_logits, layer)
        h = layer_norm(DEEPNORM_ALPHA * h + mix, ln_g[layer, 0], ln_b[layer, 0])
        ffn = grouped_moe(h, router_w, router_b, moe_w_in[layer], moe_w_out[layer])
        h = layer_norm(DEEPNORM_ALPHA * h + ffn, ln_g[layer, 1], ln_b[layer, 1])
    return h
```

```python
import functools

import numpy as np
import jax
import jax.numpy as jnp
from jax import lax
from jax.experimental import pallas as pl
from jax.experimental.pallas import tpu as pltpu

F32 = jnp.float32
BF16 = jnp.bfloat16
I32 = jnp.int32

LANE = 128
SUBLANE = 8
VMEM_LIMIT_BYTES = 56 * 2**20

RG_C = 8.0
N_GROUPS = 4
LN_EPS = 1e-5
RMS_EPS = 1e-6
HG_CHUNK = 128
HG_BAND = 8


def _params(*sem):
    return pltpu.CompilerParams(dimension_semantics=sem, vmem_limit_bytes=VMEM_LIMIT_BYTES)


def _round_up(x, m):
    return (x + m - 1) // m * m


def _silu(x):
    return x * jax.nn.sigmoid(x)


def _softplus(x):
    return jnp.maximum(x, 0.0) + jnp.log1p(jnp.exp(-jnp.abs(x)))


def _dot(a, b):
    return jnp.dot(a, b, preferred_element_type=F32)


def _dot_nt(a, b):
    return lax.dot_general(a, b, (((1,), (1,)), ((), ())), preferred_element_type=F32)


def _dot_tn(a, b):
    return lax.dot_general(a, b, (((0,), (0,)), ((), ())), preferred_element_type=F32)


def _split_bf16(x):
    hi = x.astype(BF16)
    lo = (x - hi.astype(F32)).astype(BF16)
    return hi, lo


def _mm_body(x_ref, w_ref, o_ref, *, acts, tiles_per_region):
    acc = _dot(x_ref[...], w_ref[...])
    if len(acts) == 1:
        o_ref[...] = acts[0](acc).astype(o_ref.dtype)
        return
    region = pl.program_id(1) // tiles_per_region
    for r, act in enumerate(acts):
        @pl.when(region == r)
        def _(act=act):
            o_ref[...] = act(acc).astype(o_ref.dtype)


def _matmul(x, w, *, acts, out_dtype, tm, tn, name):
    m, k = x.shape
    n = w.shape[1]
    region = n // len(acts)
    tm, tn = min(tm, m), min(tn, region)
    assert m % tm == 0 and region * len(acts) == n and region % tn == 0
    return pl.pallas_call(
        functools.partial(_mm_body, acts=acts, tiles_per_region=region // tn),
        grid=(m // tm, n // tn),
        in_specs=[pl.BlockSpec((tm, k), lambda i, j: (i, 0)),
                  pl.BlockSpec((k, tn), lambda i, j: (0, j))],
        out_specs=pl.BlockSpec((tm, tn), lambda i, j: (i, j)),
        out_shape=jax.ShapeDtypeStruct((m, n), out_dtype),
        compiler_params=_params("arbitrary", "arbitrary"),
        name=name,
    )(x, w)


def _hg_gate_body(x_ref, w_ref, lbl_ref, lf_ref, k_ref, *, layer):
    z = _dot(x_ref[...], w_ref[...])
    logits = lbl_ref[...]
    mx = jnp.max(logits, axis=0, keepdims=True)
    ex = jnp.exp(logits - mx)
    probs = ex / jnp.sum(ex, axis=0, keepdims=True)
    lb = jnp.zeros_like(mx)
    for l in range(1, layer + 1):
        lb = lb + probs[l:l + 1, :]
    log_sig = jnp.minimum(z, 0.0) - jnp.log1p(jnp.exp(-jnp.abs(z)))
    la = jnp.log(lb)
    lc = jnp.log1p(-lb) + log_sig
    lf_ref[...] = jnp.maximum(la, lc) + jnp.log1p(jnp.exp(-jnp.abs(la - lc)))
    k_ref[...] = ((1.0 - lb) * jax.nn.sigmoid(-z)).astype(k_ref.dtype)


def _hg_gate_proj(x, w, lb_logits, *, layer, tm, tn):
    m, k = x.shape
    n = w.shape[1]
    depth = lb_logits.shape[0]
    tm, tn = min(tm, m), min(tn, n)
    assert m % tm == 0 and n % tn == 0
    return pl.pallas_call(
        functools.partial(_hg_gate_body, layer=layer),
        grid=(m // tm, n // tn),
        in_specs=[pl.BlockSpec((tm, k), lambda i, j: (i, 0)),
                  pl.BlockSpec((k, tn), lambda i, j: (0, j)),
                  pl.BlockSpec((depth, tn), lambda i, j: (0, j))],
        out_specs=[pl.BlockSpec((tm, tn), lambda i, j: (i, j)),
                   pl.BlockSpec((tm, tn), lambda i, j: (i, j))],
        out_shape=[jax.ShapeDtypeStruct((m, n), F32), jax.ShapeDtypeStruct((m, n), BF16)],
        compiler_params=_params("arbitrary", "arbitrary"),
        name="hg_gate_proj",
    )(x, w, lb_logits)


def _rg_body(gate_ref, xr_ref, cw_ref, cb_ref, wra_ref, bra_ref, wix_ref, bix_ref, lam_ref,
             y_ref, xpad, a_scr, u_scr, hcar, *, tt, conv_w):
    @pl.when(pl.program_id(2) == 0)
    def _():
        xpad[0:SUBLANE, :] = jnp.zeros((SUBLANE, xpad.shape[1]), F32)
        hcar[...] = jnp.zeros_like(hcar)

    x = xr_ref[...].astype(F32)
    xpad[SUBLANE:SUBLANE + tt, :] = x
    xc = cb_ref[...] + cw_ref[conv_w - 1:conv_w, :] * x
    for j in range(conv_w - 1):
        xc = xc + cw_ref[j:j + 1, :] * xpad[pl.ds(SUBLANE - (conv_w - 1) + j, tt), :]
    xpad[0:SUBLANE, :] = x[tt - SUBLANE:tt, :]

    xcb = xc.astype(BF16)
    r = jax.nn.sigmoid(_dot(xcb, wra_ref[...]) + bra_ref[...])
    ig = jax.nn.sigmoid(_dot(xcb, wix_ref[...]) + bix_ref[...])
    log_a = (-RG_C * _softplus(-lam_ref[...])) * r
    a = jnp.exp(log_a)
    a_scr[...] = a
    u_scr[...] = jnp.sqrt(-jnp.tanh(log_a) * (a * a + 1.0)) * (ig * xc)

    row = lax.broadcasted_iota(I32, (SUBLANE, xpad.shape[1]), 0)

    def block(i, hprev):
        r0 = pl.multiple_of(i * SUBLANE, SUBLANE)
        a = a_scr[pl.ds(r0, SUBLANE), :]
        u = u_scr[pl.ds(r0, SUBLANE), :]
        for s in (1, 2, 4):
            keep = row >= s
            u = jnp.where(keep, a * pltpu.roll(u, s, axis=0) + u, u)
            a = jnp.where(keep, a * pltpu.roll(a, s, axis=0), a)
        h = a * hprev + u
        u_scr[pl.ds(r0, SUBLANE), :] = h
        return jnp.broadcast_to(h[SUBLANE - 1:SUBLANE, :], h.shape)

    hcar[...] = lax.fori_loop(0, tt // SUBLANE, block, hcar[...])
    y_ref[...] = (u_scr[...] * gate_ref[...].astype(F32)).astype(y_ref.dtype)


def _rg_core(proj, conv_w, conv_b, w_ra, b_ra, w_ix, b_ix, lam, *, batch, seq, tt):
    nh, hd = w_ra.shape[0], w_ra.shape[1]
    cw = conv_w.shape[0]
    tt = min(tt, seq)
    assert seq % tt == 0 and tt % SUBLANE == 0 and cw - 1 <= SUBLANE
    proj3 = proj.reshape(batch, seq, 2 * nh * hd)
    vec = lambda: pl.BlockSpec((1, hd), lambda b, h, t: (0, h))
    y = pl.pallas_call(
        functools.partial(_rg_body, tt=tt, conv_w=cw),
        grid=(batch, nh, seq // tt),
        in_specs=[pl.BlockSpec((None, tt, hd), lambda b, h, t: (b, t, h)),
                  pl.BlockSpec((None, tt, hd), lambda b, h, t: (b, t, nh + h)),
                  pl.BlockSpec((cw, hd), lambda b, h, t: (0, h)),
                  vec(),
                  pl.BlockSpec((None, hd, hd), lambda b, h, t: (h, 0, 0)),
                  vec(),
                  pl.BlockSpec((None, hd, hd), lambda b, h, t: (h, 0, 0)),
                  vec(),
                  vec()],
        out_specs=pl.BlockSpec((None, tt, hd), lambda b, h, t: (b, t, h)),
        out_shape=jax.ShapeDtypeStruct((batch, seq, nh * hd), BF16),
        scratch_shapes=[pltpu.VMEM((tt + SUBLANE, hd), F32),
                        pltpu.VMEM((tt, hd), F32),
                        pltpu.VMEM((tt, hd), F32),
                        pltpu.VMEM((SUBLANE, hd), F32)],
        compiler_params=_params("arbitrary", "arbitrary", "arbitrary"),
        name="rg_core",
    )(proj3, proj3, conv_w, conv_b, w_ra, b_ra, w_ix, b_ix, lam)
    return y.reshape(batch * seq, nh * hd)


def _hg_masks(chunk, band):
    t = np.arange(chunk)[:, None]
    s = np.arange(chunk)[None, :]
    masks = []
    half = chunk // 2
    while half >= band:
        same_parent = (t // (2 * half)) == (s // (2 * half))
        split = ((t // half) % 2 == 1) & ((s // half) % 2 == 0)
        masks.append((same_parent & split & (t - s >= band)).astype(np.float32))
        half //= 2
    return np.stack(masks)


def _hg_body(q_ref, v_ref, g_ref, lf_ref, k_ref, nw_ref, tri_ref, mask_ref, o_ref,
             st_ref, kpad, bpad, vpad, *, tt, chunk, band):
    kd = q_ref.shape[1]

    @pl.when(pl.program_id(2) == 0)
    def _():
        st_ref[...] = jnp.zeros_like(st_ref)

    zpad = jnp.zeros((SUBLANE, kd), F32)
    kpad[0:SUBLANE, :] = zpad
    bpad[0:SUBLANE, :] = zpad
    vpad[0:SUBLANE, :] = zpad
    tri = tri_ref[...]
    row = lax.broadcasted_iota(I32, (chunk, 1), 0)

    def one_chunk(c, carry):
        r0 = pl.multiple_of(c * chunk, chunk)
        q = q_ref[pl.ds(r0, chunk), :].astype(F32)
        k = k_ref[pl.ds(r0, chunk), :].astype(F32)
        vb = v_ref[pl.ds(r0, chunk), :]
        v = vb.astype(F32)
        lf_hi, lf_lo = _split_bf16(lf_ref[pl.ds(r0, chunk), :])
        b = _dot(tri, lf_hi) + _dot(tri, lf_lo)
        b_last = b[chunk - 1:chunk, :]

        st = st_ref[...]
        o = _dot_nt((q * jnp.exp(b)).astype(BF16), st.astype(BF16))
        k_dec = (k * jnp.exp(b_last - b)).astype(BF16)
        st_ref[...] = st * jnp.exp(b_last) + _dot_tn(vb, k_dec)

        scores = jnp.zeros((chunk, chunk), F32)
        half, level = chunk // 2, 0
        while half >= band:
            ref_rows = [jnp.broadcast_to(b[p + half - 1:p + half, :], (2 * half, kd))
                        for p in range(0, chunk, 2 * half)]
            ref = ref_rows[0] if len(ref_rows) == 1 else jnp.concatenate(ref_rows, axis=0)
            qt = (q * jnp.exp(jnp.minimum(b - ref, 0.0))).astype(BF16)
            kt = (k * jnp.exp(jnp.minimum(ref - b, 0.0))).astype(BF16)
            scores = scores + mask_ref[level] * _dot_nt(qt, kt)
            half //= 2
            level += 1
        o = o + _dot(scores.astype(BF16), vb)

        kpad[SUBLANE:SUBLANE + chunk, :] = k
        bpad[SUBLANE:SUBLANE + chunk, :] = b
        vpad[SUBLANE:SUBLANE + chunk, :] = v
        for d in range(band):
            kd_ = kpad[pl.ds(SUBLANE - d, chunk), :]
            bd_ = bpad[pl.ds(SUBLANE - d, chunk), :]
            vd_ = vpad[pl.ds(SUBLANE - d, chunk), :]
            w = q * kd_ * jnp.exp(jnp.minimum(b - bd_, 0.0))
            sc = jnp.sum(w, axis=-1, keepdims=True)
            o = o + jnp.where(row >= d, sc, 0.0) * vd_

        ms = jnp.mean(o * o, axis=-1, keepdims=True)
        y = o * lax.rsqrt(ms + RMS_EPS) * nw_ref[...]
        o_ref[pl.ds(r0, chunk), :] = (y * g_ref[pl.ds(r0, chunk), :].astype(F32)).astype(o_ref.dtype)
        return carry

    lax.fori_loop(0, tt // chunk, one_chunk, 0)


def _hg_core(qvg, log_f, key, norm_w, *, batch, seq, heads, tt):
    kd = norm_w.shape[-1]
    assert kd == LANE, "head key/value width must equal the lane width"
    tt = min(tt, seq)
    chunk = min(HG_CHUNK, tt)
    assert seq % tt == 0 and tt % chunk == 0 and chunk % (2 * HG_BAND) == 0
    masks = jnp.asarray(_hg_masks(chunk, HG_BAND))
    tri = jnp.asarray(np.tril(np.ones((chunk, chunk), np.float32)), dtype=BF16)
    qvg3 = qvg.reshape(batch, seq, 3 * heads * kd)
    lf3 = log_f.reshape(batch, seq, heads * kd)
    k3 = key.reshape(batch, seq, heads * kd)
    blk = lambda off: pl.BlockSpec((None, tt, kd), lambda b, h, t, off=off: (b, t, off + h))
    out = pl.pallas_call(
        functools.partial(_hg_body, tt=tt, chunk=chunk, band=HG_BAND),
        grid=(batch, heads, seq // tt),
        in_specs=[blk(0), blk(heads), blk(2 * heads), blk(0), blk(0),
                  pl.BlockSpec((1, kd), lambda b, h, t: (0, 0)),
                  pl.BlockSpec((chunk, chunk), lambda b, h, t: (0, 0)),
                  pl.BlockSpec(masks.shape, lambda b, h, t: (0, 0, 0))],
        out_specs=blk(0),
        out_shape=jax.ShapeDtypeStruct((batch, seq, heads * kd), BF16),
        scratch_shapes=[pltpu.VMEM((kd, kd), F32),
                        pltpu.VMEM((chunk + SUBLANE, kd), F32),
                        pltpu.VMEM((chunk + SUBLANE, kd), F32),
                        pltpu.VMEM((chunk + SUBLANE, kd), F32)],
        compiler_params=_params("arbitrary", "arbitrary", "arbitrary"),
        name="hg_core",
    )(qvg3, qvg3, qvg3, lf3, k3, norm_w, tri, masks)
    return out.reshape(batch * seq, heads * kd)


def _layer_norm(x, g, b):
    mu = jnp.mean(x, axis=-1, keepdims=True)
    xc = x - mu
    var = jnp.mean(xc * xc, axis=-1, keepdims=True)
    return xc * lax.rsqrt(var + LN_EPS) * g + b


def _first_max(vals):
    best, idx = vals[0], jnp.zeros(vals[0].shape, I32)
    for i in range(1, len(vals)):
        better = vals[i] > best
        best = jnp.where(better, vals[i], best)
        idx = jnp.where(better, i, idx)
    return best, idx


def _route(logit_rows, n_groups):
    n_exp = len(logit_rows)
    per = n_exp // n_groups
    mx = functools.reduce(jnp.maximum, logit_rows)
    ex = [jnp.exp(l - mx) for l in logit_rows]
    den = functools.reduce(lambda a, b: a + b, ex)
    p = [e / den for e in ex]
    scores = []
    for g in range(n_groups):
        grp = sorted_desc = p[g * per:(g + 1) * per]
        top1, i1 = _first_max(grp)
        rest = [jnp.where(i1 == i, -jnp.inf, grp[i]) for i in range(per)]
        top2, _ = _first_max(rest)
        scores.append(top1 + top2)
    _, g_sel = _first_max(scores)
    in_group = []
    for i in range(per):
        val = p[i]
        for g in range(1, n_groups):
            val = jnp.where(g_sel == g, p[g * per + i], val)
        in_group.append(val)
    p1, i1 = _first_max(in_group)
    rest = [jnp.where(i1 == i, -jnp.inf, in_group[i]) for i in range(per)]
    p2, i2 = _first_max(rest)
    tot = p1 + p2
    return g_sel * per + i1, g_sel * per + i2, p1 / tot, p2 / tot


def _ln_route_body(mix_ref, h_ref, g_ref, b_ref, rwh_ref, rwl_ref, rb_ref, o_ref, eid_ref, wt_ref,
                   *, alpha, n_exp, n_groups):
    y = _layer_norm(alpha * h_ref[...].astype(F32) + mix_ref[...], g_ref[...], b_ref[...])
    o_ref[...] = y
    y_hi, y_lo = _split_bf16(y)
    logits = _dot(y_hi, rwh_ref[...]) + (_dot(y_hi, rwl_ref[...]) + _dot(y_lo, rwh_ref[...]))
    lt = logits.T + rb_ref[...]
    e1, e2, w1, w2 = _route([lt[e:e + 1, :] for e in range(n_exp)], n_groups)
    eid_ref[0:1, :] = e1
    eid_ref[1:2, :] = e2
    wt_ref[0:1, :] = w1
    wt_ref[1:2, :] = w2


def _ln_route(mix, h, g, b, router_w, router_b, *, alpha, tr):
    n, d = mix.shape
    n_exp = router_w.shape[1]
    tr = min(tr, n)
    assert n % tr == 0 and n_exp <= LANE and n_exp % N_GROUPS == 0
    rw = jnp.zeros((d, LANE), F32).at[:, :n_exp].set(router_w.astype(F32))
    rw_hi = rw.astype(BF16)
    rw_lo = (rw - rw_hi.astype(F32)).astype(BF16)
    rb = jnp.zeros((LANE, 1), F32).at[:n_exp, 0].set(router_b.astype(F32))
    row = lambda: pl.BlockSpec((tr, d), lambda i: (i, 0))
    full = lambda shape: pl.BlockSpec(shape, lambda i: (0, 0))
    return pl.pallas_call(
        functools.partial(_ln_route_body, alpha=alpha, n_exp=n_exp, n_groups=N_GROUPS),
        grid=(n // tr,),
        in_specs=[row(), row(), full((1, d)), full((1, d)), full((d, LANE)), full((d, LANE)),
                  full((LANE, 1))],
        out_specs=[row(), pl.BlockSpec((2, tr), lambda i: (0, i)), pl.BlockSpec((2, tr), lambda i: (0, i))],
        out_shape=[jax.ShapeDtypeStruct((n, d), F32), jax.ShapeDtypeStruct((2, n), I32),
                   jax.ShapeDtypeStruct((2, n), F32)],
        compiler_params=_params("arbitrary"),
        name="ln_route",
    )(mix, h, g, b, rw_hi, rw_lo, rb)


def _gather_rows(src_hbm, idx_ref, dst, sem, n_rows):
    def copy(r):
        return pltpu.make_async_copy(src_hbm.at[pl.ds(idx_ref[0, 0, r], 1)], dst.at[pl.ds(r, 1)], sem)

    def start(r, c):
        copy(r).start()
        return c

    def wait(r, c):
        copy(r).wait()
        return c

    lax.fori_loop(0, n_rows, start, 0)
    lax.fori_loop(0, n_rows, wait, 0)


def _dispatch_body(src_ref, h_hbm, xs_ref, buf, sem, *, td):
    _gather_rows(h_hbm, src_ref, buf, sem, td)
    xs_ref[...] = buf[...].astype(xs_ref.dtype)


def _dispatch(h, src_token, *, td):
    n, d = h.shape
    p = src_token.shape[0]
    td = min(td, p)
    assert p % td == 0
    return pl.pallas_call(
        functools.partial(_dispatch_body, td=td),
        grid=(p // td,),
        in_specs=[pl.BlockSpec((1, 1, td), lambda i: (i, 0, 0), memory_space=pltpu.SMEM),
                  pl.BlockSpec(memory_space=pl.ANY)],
        out_specs=pl.BlockSpec((td, d), lambda i: (i, 0)),
        out_shape=jax.ShapeDtypeStruct((p, d), BF16),
        scratch_shapes=[pltpu.VMEM((td, d), F32), pltpu.SemaphoreType.DMA(())],
        compiler_params=_params("arbitrary"),
        name="moe_dispatch",
    )(src_token.reshape(p // td, 1, td), h)


def _moe_up_body(te_ref, nu_ref, x_ref, wg_ref, wu_ref, o_ref):
    @pl.when(pl.program_id(1) < nu_ref[0])
    def _():
        x = x_ref[...]
        o_ref[...] = (_silu(_dot(x, wg_ref[...])) * _dot(x, wu_ref[...])).astype(o_ref.dtype)

    @pl.when(pl.program_id(1) >= nu_ref[0])
    def _():
        o_ref[...] = jnp.zeros_like(o_ref)


def _moe_up(xs, w_in, tile_expert, n_used, *, tm, tn):
    p, d = xs.shape
    ff = w_in.shape[2] // 2
    tn = min(tn, ff)
    assert p % tm == 0 and ff % tn == 0
    nj = ff // tn
    return pl.pallas_call(
        _moe_up_body,
        grid_spec=pltpu.PrefetchScalarGridSpec(
            num_scalar_prefetch=2,
            grid=(nj, p // tm),
            in_specs=[pl.BlockSpec((tm, d), lambda j, i, te, nu: (i, 0)),
                      pl.BlockSpec((None, d, tn), lambda j, i, te, nu: (te[i], 0, j)),
                      pl.BlockSpec((None, d, tn), lambda j, i, te, nu: (te[i], 0, nj + j))],
            out_specs=pl.BlockSpec((tm, tn), lambda j, i, te, nu: (i, j))),
        out_shape=jax.ShapeDtypeStruct((p, ff), BF16),
        compiler_params=_params("arbitrary", "arbitrary"),
        name="moe_up",
    )(tile_expert, n_used, xs, w_in, w_in)


def _moe_down_body(te_ref, nu_ref, x_ref, w_ref, o_ref):
    @pl.when(pl.program_id(0) < nu_ref[0])
    def _():
        o_ref[...] = _dot(x_ref[...], w_ref[...])

    @pl.when(pl.program_id(0) >= nu_ref[0])
    def _():
        o_ref[...] = jnp.zeros_like(o_ref)


def _moe_down(h1, w_out, tile_expert, n_used, *, tm):
    p, ff = h1.shape
    d = w_out.shape[2]
    assert p % tm == 0
    return pl.pallas_call(
        _moe_down_body,
        grid_spec=pltpu.PrefetchScalarGridSpec(
            num_scalar_prefetch=2,
            grid=(p // tm,),
            in_specs=[pl.BlockSpec((tm, ff), lambda i, te, nu: (i, 0)),
                      pl.BlockSpec((None, ff, d), lambda i, te, nu: (te[i], 0, 0))],
            out_specs=pl.BlockSpec((tm, d), lambda i, te, nu: (i, 0))),
        out_shape=jax.ShapeDtypeStruct((p, d), F32),
        compiler_params=_params("arbitrary"),
        name="moe_down",
    )(tile_expert, n_used, h1, w_out)


def _combine_ln_body(pos_ref, ys_hbm, h_ref, wt_ref, g_ref, b_ref, o_ref, buf, sem, *, tc, alpha):
    _gather_rows(ys_hbm, pos_ref, buf, sem, 2 * tc)
    wt = wt_ref[...]
    ffn = wt[:, 0:1] * buf[0:tc, :] + wt[:, 1:2] * buf[tc:2 * tc, :]
    o_ref[...] = _layer_norm(alpha * h_ref[...] + ffn, g_ref[...], b_ref[...]).astype(o_ref.dtype)


def _combine_ln(ys, pos, h, wts, g, b, *, alpha, out_dtype, tc):
    n, d = h.shape
    tc = min(tc, n)
    assert n % tc == 0
    pos_tiles = jnp.concatenate([pos[0].reshape(n // tc, 1, tc), pos[1].reshape(n // tc, 1, tc)], axis=2)
    return pl.pallas_call(
        functools.partial(_combine_ln_body, tc=tc, alpha=alpha),
        grid=(n // tc,),
        in_specs=[pl.BlockSpec((1, 1, 2 * tc), lambda i: (i, 0, 0), memory_space=pltpu.SMEM),
                  pl.BlockSpec(memory_space=pl.ANY),
                  pl.BlockSpec((tc, d), lambda i: (i, 0)),
                  pl.BlockSpec((tc, 2), lambda i: (i, 0)),
                  pl.BlockSpec((1, d), lambda i: (0, 0)),
                  pl.BlockSpec((1, d), lambda i: (0, 0))],
        out_specs=pl.BlockSpec((tc, d), lambda i: (i, 0)),
        out_shape=jax.ShapeDtypeStruct((n, d), out_dtype),
        scratch_shapes=[pltpu.VMEM((2 * tc, d), F32), pltpu.SemaphoreType.DMA(())],
        compiler_params=_params("arbitrary"),
        name="moe_combine_ln",
    )(pos_tiles, ys, h, wts.T, g, b)


def _moe_plan(eid, n_exp, tm):
    n = eid.shape[1]
    flat = eid.reshape(-1)
    onehot = (flat[:, None] == jnp.arange(n_exp, dtype=I32)[None, :]).astype(I32)
    csum = jnp.cumsum(onehot, axis=0)
    rank = jnp.take_along_axis(csum, flat[:, None], axis=1)[:, 0] - 1
    cnt = csum[-1]
    tiles = (cnt + tm - 1) // tm
    tile_end = jnp.cumsum(tiles)
    tile_start = tile_end - tiles
    pos = (tile_start[flat] * tm + rank).reshape(2, n)

    n_tiles = (2 * n) // tm + n_exp
    n_used = tile_end[-1]
    tile_id = jnp.arange(n_tiles, dtype=I32)
    te = jnp.minimum(jnp.searchsorted(tile_end, tile_id, side="right"), n_exp - 1).astype(I32)
    te = jnp.where(tile_id < n_used, te, te[jnp.maximum(n_used - 1, 0)])

    order = jnp.argsort(flat, stable=True).astype(I32)
    grp_start = jnp.cumsum(cnt) - cnt
    row = jnp.arange(n_tiles * tm, dtype=I32)
    e_row = te[row // tm]
    r_in = row - tile_start[e_row] * tm
    src = order[jnp.clip(grp_start[e_row] + jnp.minimum(r_in, cnt[e_row] - 1), 0, 2 * n - 1)] % n
    return pos.astype(I32), src.astype(I32), te, n_used.astype(I32).reshape(1)


def _moe_ln(h, eid, wts, w_in, w_out, g, b, *, alpha, out_dtype, tm):
    n_exp = w_in.shape[0]
    pos, src, tile_expert, n_used = _moe_plan(eid, n_exp, tm)
    xs = _dispatch(h, src, td=tm)
    h1 = _moe_up(xs, w_in, tile_expert, n_used, tm=tm, tn=768)
    ys = _moe_down(h1, w_out, tile_expert, n_used, tm=tm)
    return _combine_ln(ys, pos, h, wts, g, b, alpha=alpha, out_dtype=out_dtype, tc=256)


def _pad_heads(a, nh, hd, hdp, axis):
    shape = a.shape
    a = a.reshape(shape[:axis] + (nh, hd) + shape[axis + 1:])
    pad = [(0, 0)] * a.ndim
    pad[axis + 1] = (0, hdp - hd)
    a = jnp.pad(a, pad)
    return a.reshape(shape[:axis] + (nh * hdp,) + shape[axis + 1:])


def kernel(x, rg_w_in, rg_conv_w, rg_conv_b, rg_w_ra, rg_b_ra, rg_w_ix, rg_b_ix, rg_lambda, rg_w_out,
           hg_w_in, hg_norm_w, hg_w_out, hg_lb_logits, router_w, router_b, moe_w_in, moe_w_out, ln_g, ln_b):
    batch, seq, d = x.shape
    n = batch * seq
    depth = ln_g.shape[0]
    alpha = (2.0 * depth) ** 0.25
    nh, hd = rg_w_ra.shape[1], rg_w_ra.shape[2]
    hdp = _round_up(hd, LANE)
    hv = hg_w_out.shape[1]
    hk = (hg_w_in.shape[2] - 2 * hv) // 2
    heads = hv // hg_norm_w.shape[1]
    assert hk == hv, "HGRN2 key and value widths must match"
    moe_tm = min(512, n)

    h = x.reshape(n, d)
    h_bf = h.astype(BF16)
    for layer in range(depth):
        j = layer // 2
        if layer % 2 == 0:
            w_in = _pad_heads(rg_w_in[j].reshape(d, 2, nh * hd), nh, hd, hdp, 2).reshape(d, 2 * nh * hdp)
            proj = _matmul(h_bf, w_in.astype(BF16), acts=(jax.nn.gelu, lambda a: a), out_dtype=BF16,
                           tm=1024, tn=1024, name="rg_in_proj")
            pad_vec = lambda v: _pad_heads(v.reshape(1, nh * hd).astype(F32), nh, hd, hdp, 1)
            pad_mat = lambda w: jnp.pad(w, ((0, 0), (0, hdp - hd), (0, hdp - hd))).astype(BF16)
            y = _rg_core(proj, _pad_heads(rg_conv_w[j].astype(F32), nh, hd, hdp, 1), pad_vec(rg_conv_b[j]),
                         pad_mat(rg_w_ra[j]), pad_vec(rg_b_ra[j]), pad_mat(rg_w_ix[j]), pad_vec(rg_b_ix[j]),
                         pad_vec(rg_lambda[j]), batch=batch, seq=seq, tt=512)
            w_out = _pad_heads(rg_w_out[j], nh, hd, hdp, 0).astype(BF16)
            mix = _matmul(y, w_out, acts=(lambda a: a,), out_dtype=F32, tm=512, tn=1024, name="rg_out_proj")
        else:
            w = hg_w_in[j]
            w_qvg = jnp.concatenate([w[:, :hk], w[:, 2 * hk:2 * hk + hv], w[:, 2 * hk + hv:]], axis=1)
            qvg = _matmul(h_bf, w_qvg.astype(BF16), acts=(_silu, lambda a: a, _silu), out_dtype=BF16,
                          tm=1024, tn=1024, name="hg_in_proj")
            log_f, key = _hg_gate_proj(h_bf, w[:, hk:2 * hk].astype(BF16), hg_lb_logits.astype(F32),
                                       layer=layer, tm=1024, tn=1024)
            o = _hg_core(qvg, log_f, key, hg_norm_w[j].reshape(1, -1).astype(F32),
                         batch=batch, seq=seq, heads=heads, tt=1024)
            mix = _matmul(o, hg_w_out[j].astype(BF16), acts=(lambda a: a,), out_dtype=F32,
                          tm=1024, tn=1024, name="hg_out_proj")
        g1, b1 = ln_g[layer, 0].reshape(1, d).astype(F32), ln_b[layer, 0].reshape(1, d).astype(F32)
        g2, b2 = ln_g[layer, 1].reshape(1, d).astype(F32), ln_b[layer, 1].reshape(1, d).astype(F32)
        h, eid, wts = _ln_route(mix, h, g1, b1, router_w, router_b, alpha=alpha, tr=256)
        h = _moe_ln(h, eid, wts, moe_w_in[layer].astype(BF16), moe_w_out[layer].astype(BF16), g2, b2,
                    alpha=alpha, out_dtype=F32, tm=moe_tm)
        h_bf = h.astype(BF16)
    return h.reshape(batch, seq, d).astype(x.dtype)
```

```python
import functools

import numpy as np
import jax
import jax.numpy as jnp
from jax import lax
from jax.experimental import pallas as pl
from jax.experimental.pallas import tpu as pltpu

F32 = jnp.float32
BF16 = jnp.bfloat16
I32 = jnp.int32
U32 = jnp.uint32

LANE = 128
SUBLANE = 8
VMEM_LIMIT_BYTES = 56 * 2**20

RG_C = 8.0
N_GROUPS = 4
LN_EPS = 1e-5
RMS_EPS = 1e-6
HG_CHUNK = 128
HG_BAND = 4
HG_HEADS_PER_STEP = 2
LOG2_E = 1.4426950408889634


def _params(*sem):
    return pltpu.CompilerParams(dimension_semantics=sem, vmem_limit_bytes=VMEM_LIMIT_BYTES)


def _round_up(x, m):
    return (x + m - 1) // m * m


def _silu(x):
    return x * jax.nn.sigmoid(x)


def _identity(x):
    return x


def _softplus(x):
    return jnp.maximum(x, 0.0) + jnp.log1p(jnp.exp(-jnp.abs(x)))


def _dot(a, b):
    return jnp.dot(a, b, preferred_element_type=F32)


def _dot_nt(a, b):
    return lax.dot_general(a, b, (((1,), (1,)), ((), ())), preferred_element_type=F32)


def _dot_tn(a, b):
    return lax.dot_general(a, b, (((0,), (0,)), ((), ())), preferred_element_type=F32)


def _split_bf16(x):
    hi = x.astype(BF16)
    lo = (x - hi.astype(F32)).astype(BF16)
    return hi, lo


def _pack_pairs(y):
    half = y.shape[1] // 2
    bits = lax.bitcast_convert_type(y.astype(BF16).astype(F32), U32)
    return bits[:, :half] | (bits[:, half:] >> 16)


def _unpack_pairs(p):
    hi = lax.bitcast_convert_type(p & jnp.uint32(0xFFFF0000), F32)
    lo = lax.bitcast_convert_type(p << 16, F32)
    return jnp.concatenate([hi, lo], axis=1)


def _mm_body(x_ref, w_ref, o_ref, *, acts, tiles_per_region):
    acc = _dot(x_ref[...], w_ref[...])
    if len(acts) == 1:
        o_ref[...] = acts[0](acc).astype(o_ref.dtype)
        return
    region = pl.program_id(1) // tiles_per_region
    for r, act in enumerate(acts):
        @pl.when(region == r)
        def _(act=act):
            o_ref[...] = act(acc).astype(o_ref.dtype)


def _matmul(x, w, *, acts, col_starts, width, out_dtype, tm, tn, name):
    m, k = x.shape
    tm, tn = min(tm, m), min(tn, width)
    assert m % tm == 0 and width % tn == 0 and all(c % tn == 0 for c in col_starts)
    per = width // tn
    starts = [c // tn for c in col_starts]

    def w_index(i, j):
        col = starts[0] + j
        for r in range(1, len(starts)):
            col = jnp.where(j >= r * per, starts[r] + (j - r * per), col)
        return (0, col)

    return pl.pallas_call(
        functools.partial(_mm_body, acts=acts, tiles_per_region=per),
        grid=(m // tm, per * len(acts)),
        in_specs=[pl.BlockSpec((tm, k), lambda i, j: (i, 0)),
                  pl.BlockSpec((k, tn), w_index)],
        out_specs=pl.BlockSpec((tm, tn), lambda i, j: (i, j)),
        out_shape=jax.ShapeDtypeStruct((m, width * len(acts)), out_dtype),
        compiler_params=_params("arbitrary", "arbitrary"),
        name=name,
    )(x, w)


def _hg_gate_body(x_ref, w_ref, lbl_ref, lf_ref, k_ref, *, layer):
    z = _dot(x_ref[...], w_ref[...])
    logits = lbl_ref[...]
    mx = jnp.max(logits, axis=0, keepdims=True)
    ex = jnp.exp(logits - mx)
    probs = ex / jnp.sum(ex, axis=0, keepdims=True)
    lb = jnp.zeros_like(mx)
    for l in range(1, layer + 1):
        lb = lb + probs[l:l + 1, :]
    log_sig = jnp.minimum(z, 0.0) - jnp.log1p(jnp.exp(-jnp.abs(z)))
    la = jnp.log(lb)
    lc = jnp.log1p(-lb) + log_sig
    lf_ref[...] = LOG2_E * (jnp.maximum(la, lc) + jnp.log1p(jnp.exp(-jnp.abs(la - lc))))
    k_ref[...] = ((1.0 - lb) * jax.nn.sigmoid(-z)).astype(k_ref.dtype)


def _hg_gate_proj(x, w, lb_logits, *, col_start, width, layer, tm, tn):
    m, k = x.shape
    depth = lb_logits.shape[0]
    tm, tn = min(tm, m), min(tn, width)
    assert m % tm == 0 and width % tn == 0 and col_start % tn == 0
    off = col_start // tn
    return pl.pallas_call(
        functools.partial(_hg_gate_body, layer=layer),
        grid=(m // tm, width // tn),
        in_specs=[pl.BlockSpec((tm, k), lambda i, j: (i, 0)),
                  pl.BlockSpec((k, tn), lambda i, j: (0, off + j)),
                  pl.BlockSpec((depth, tn), lambda i, j: (0, j))],
        out_specs=[pl.BlockSpec((tm, tn), lambda i, j: (i, j)),
                   pl.BlockSpec((tm, tn), lambda i, j: (i, j))],
        out_shape=[jax.ShapeDtypeStruct((m, width), F32), jax.ShapeDtypeStruct((m, width), BF16)],
        compiler_params=_params("arbitrary", "arbitrary"),
        name="hg_gate_proj",
    )(x, w, lb_logits)


def _rg_body(gate_ref, xr_ref, cw_ref, cb_ref, wra_ref, bra_ref, wix_ref, bix_ref, lam_ref,
             y_ref, xpad, a_scr, u_scr, hcar, *, tt, conv_w):
    @pl.when(pl.program_id(2) == 0)
    def _():
        xpad[0:SUBLANE, :] = jnp.zeros((SUBLANE, xpad.shape[1]), F32)
        hcar[...] = jnp.zeros_like(hcar)

    x = xr_ref[...].astype(F32)
    xpad[SUBLANE:SUBLANE + tt, :] = x
    xc = cb_ref[...] + cw_ref[conv_w - 1:conv_w, :] * x
    for j in range(conv_w - 1):
        xc = xc + cw_ref[j:j + 1, :] * xpad[pl.ds(SUBLANE - (conv_w - 1) + j, tt), :]
    xpad[0:SUBLANE, :] = x[tt - SUBLANE:tt, :]

    xcb = xc.astype(BF16)
    r = jax.nn.sigmoid(_dot(xcb, wra_ref[...]) + bra_ref[...])
    ig = jax.nn.sigmoid(_dot(xcb, wix_ref[...]) + bix_ref[...])
    log_a = (-RG_C * _softplus(-lam_ref[...])) * r
    a = jnp.exp(log_a)
    a_scr[...] = a
    u_scr[...] = jnp.sqrt(-jnp.tanh(log_a) * (a * a + 1.0)) * (ig * xc)

    row = lax.broadcasted_iota(I32, (SUBLANE, xpad.shape[1]), 0)

    def block(i, hprev):
        r0 = pl.multiple_of(i * SUBLANE, SUBLANE)
        a = a_scr[pl.ds(r0, SUBLANE), :]
        u = u_scr[pl.ds(r0, SUBLANE), :]
        for s in (1, 2, 4):
            keep = row >= s
            u = jnp.where(keep, a * pltpu.roll(u, s, axis=0) + u, u)
            a = jnp.where(keep, a * pltpu.roll(a, s, axis=0), a)
        h = a * hprev + u
        u_scr[pl.ds(r0, SUBLANE), :] = h
        return jnp.broadcast_to(h[SUBLANE - 1:SUBLANE, :], h.shape)

    hcar[...] = lax.fori_loop(0, tt // SUBLANE, block, hcar[...])
    y_ref[...] = (u_scr[...] * gate_ref[...].astype(F32)).astype(y_ref.dtype)


def _rg_core(proj, conv_w, conv_b, w_ra, b_ra, w_ix, b_ix, lam, *, batch, seq, tt):
    nh, hd = w_ra.shape[0], w_ra.shape[1]
    cw = conv_w.shape[0]
    tt = min(tt, seq)
    assert seq % tt == 0 and tt % SUBLANE == 0 and cw - 1 <= SUBLANE
    proj3 = proj.reshape(batch, seq, 2 * nh * hd)
    vec = lambda: pl.BlockSpec((1, hd), lambda b, h, t: (0, h))
    y = pl.pallas_call(
        functools.partial(_rg_body, tt=tt, conv_w=cw),
        grid=(batch, nh, seq // tt),
        in_specs=[pl.BlockSpec((None, tt, hd), lambda b, h, t: (b, t, h)),
                  pl.BlockSpec((None, tt, hd), lambda b, h, t: (b, t, nh + h)),
                  pl.BlockSpec((cw, hd), lambda b, h, t: (0, h)),
                  vec(),
                  pl.BlockSpec((None, hd, hd), lambda b, h, t: (h, 0, 0)),
                  vec(),
                  pl.BlockSpec((None, hd, hd), lambda b, h, t: (h, 0, 0)),
                  vec(),
                  vec()],
        out_specs=pl.BlockSpec((None, tt, hd), lambda b, h, t: (b, t, h)),
        out_shape=jax.ShapeDtypeStruct((batch, seq, nh * hd), BF16),
        scratch_shapes=[pltpu.VMEM((tt + SUBLANE, hd), F32),
                        pltpu.VMEM((tt, hd), F32),
                        pltpu.VMEM((tt, hd), F32),
                        pltpu.VMEM((SUBLANE, hd), F32)],
        compiler_params=_params("arbitrary", "arbitrary", "arbitrary"),
        name="rg_core",
    )(proj3, proj3, conv_w, conv_b, w_ra, b_ra, w_ix, b_ix, lam)
    return y.reshape(batch * seq, nh * hd)


def _hg_tables(chunk, band, width):
    t = np.arange(chunk)[:, None]
    s = np.arange(chunk)[None, :]
    masks, signs = [], []
    half = chunk // 2
    while half >= band:
        same_parent = (t // (2 * half)) == (s // (2 * half))
        split = ((t // half) % 2 == 1) & ((s // half) % 2 == 0)
        masks.append((same_parent & split & (t - s >= band)).astype(np.float32))
        signs.append(np.broadcast_to(np.where((t // half) % 2 == 1, 1.0, -1.0), (chunk, width)).astype(np.float32))
        half //= 2
    return np.stack(masks), np.stack(signs)


def _hg_body(q_ref, v_ref, g_ref, lf_ref, k_ref, nw_ref, tri_ref, mask_ref, sign_ref, o_ref,
             st_ref, kpad, bpad, vpad, *, tt, chunk, band, heads):
    kd = LANE

    @pl.when(pl.program_id(2) == 0)
    def _():
        st_ref[...] = jnp.zeros_like(st_ref)

    zpad = jnp.zeros((SUBLANE, heads * kd), F32)
    kpad[0:SUBLANE, :] = zpad
    bpad[0:SUBLANE, :] = zpad
    vpad[0:SUBLANE, :] = zpad
    tri = tri_ref[...]
    row = lax.broadcasted_iota(I32, (chunk, 1), 0)

    def one_chunk(c, carry):
        r0 = pl.multiple_of(c * chunk, chunk)
        q2 = q_ref[pl.ds(r0, chunk), :].astype(F32)
        k2 = k_ref[pl.ds(r0, chunk), :].astype(F32)
        vb2 = v_ref[pl.ds(r0, chunk), :]
        g2 = g_ref[pl.ds(r0, chunk), :].astype(F32)
        lf_hi, lf_lo = _split_bf16(lf_ref[pl.ds(r0, chunk), :])
        b2 = _dot(tri, lf_hi) + _dot(tri, lf_lo)
        kpad[SUBLANE:SUBLANE + chunk, :] = k2
        bpad[SUBLANE:SUBLANE + chunk, :] = b2
        vpad[SUBLANE:SUBLANE + chunk, :] = vb2.astype(F32)

        outs = []
        for h in range(heads):
            sl = slice(h * kd, (h + 1) * kd)
            q, k, vb, b = q2[:, sl], k2[:, sl], vb2[:, sl], b2[:, sl]
            b_last = b[chunk - 1:chunk, :]

            st = st_ref[h]
            o = _dot_nt((q * jnp.exp2(b)).astype(BF16), st.astype(BF16))
            k_dec = (k * jnp.exp2(b_last - b)).astype(BF16)
            st_ref[h] = st * jnp.exp2(b_last) + _dot_tn(vb, k_dec)

            scores = jnp.zeros((chunk, chunk), BF16)
            half, level = chunk // 2, 0
            while half >= band:
                if half >= SUBLANE:
                    rows = []
                    for p0 in range(0, chunk, 2 * half):
                        r = b[p0 + half - 1:p0 + half, :]
                        lo_, mid_, hi_ = p0, p0 + half, p0 + 2 * half
                        rows.append(k[lo_:mid_] * jnp.exp2(r - b[lo_:mid_]))
                        rows.append(q[mid_:hi_] * jnp.exp2(b[mid_:hi_] - r))
                    both = jnp.concatenate(rows, axis=0).astype(BF16)
                else:
                    b3 = b.reshape(chunk // (2 * half), 2 * half, kd)
                    ref = jnp.broadcast_to(b3[:, half - 1:half, :], b3.shape).reshape(chunk, kd)
                    sign = sign_ref[level]
                    e = jnp.exp2((b - ref) * sign)
                    both = (jnp.where(sign > 0.0, q, k) * e).astype(BF16)
                scores = scores + mask_ref[level] * _dot_nt(both, both).astype(BF16)
                half //= 2
                level += 1
            o = o + _dot(scores, vb)

            o = o + jnp.sum(q * k, axis=-1, keepdims=True) * vb.astype(F32)
            for d in range(1, band):
                kd_ = kpad[pl.ds(SUBLANE - d, chunk), sl]
                bd_ = bpad[pl.ds(SUBLANE - d, chunk), sl]
                vd_ = vpad[pl.ds(SUBLANE - d, chunk), sl]
                w = q * kd_ * jnp.exp2(jnp.minimum(b - bd_, 0.0))
                sc = jnp.sum(w, axis=-1, keepdims=True)
                o = o + jnp.where(row >= d, sc, 0.0) * vd_

            ms = jnp.mean(o * o, axis=-1, keepdims=True)
            outs.append(o * lax.rsqrt(ms + RMS_EPS) * nw_ref[...] * g2[:, sl])
        y = outs[0] if heads == 1 else jnp.concatenate(outs, axis=1)
        o_ref[pl.ds(r0, chunk), :] = y.astype(o_ref.dtype)
        return carry

    lax.fori_loop(0, tt // chunk, one_chunk, 0, unroll=4)


def _hg_core(qvg, log_f, key, norm_w, *, batch, seq, heads, tt):
    kd = norm_w.shape[-1]
    assert kd == LANE, "head key/value width must equal the lane width"
    hps = HG_HEADS_PER_STEP if heads % HG_HEADS_PER_STEP == 0 else 1
    tt = min(tt, seq)
    chunk = min(HG_CHUNK, tt)
    assert seq % tt == 0 and tt % chunk == 0 and chunk % (2 * SUBLANE) == 0 and chunk & (chunk - 1) == 0
    masks_np, signs_np = _hg_tables(chunk, HG_BAND, kd)
    masks = jnp.asarray(masks_np, dtype=BF16)
    signs = jnp.asarray(signs_np)
    tri = jnp.asarray(np.tril(np.ones((chunk, chunk), np.float32)), dtype=BF16)
    wd = hps * kd
    groups = heads // hps
    qvg3 = qvg.reshape(batch, seq, 3 * heads * kd)
    lf3 = log_f.reshape(batch, seq, heads * kd)
    k3 = key.reshape(batch, seq, heads * kd)
    blk = lambda off: pl.BlockSpec((None, tt, wd), lambda b, h, t, off=off: (b, t, off + h))
    const = lambda a: pl.BlockSpec(a.shape, lambda b, h, t: (0,) * a.ndim)
    out = pl.pallas_call(
        functools.partial(_hg_body, tt=tt, chunk=chunk, band=HG_BAND, heads=hps),
        grid=(batch, groups, seq // tt),
        in_specs=[blk(0), blk(groups), blk(2 * groups), blk(0), blk(0),
                  const(norm_w), const(tri), const(masks), const(signs)],
        out_specs=blk(0),
        out_shape=jax.ShapeDtypeStruct((batch, seq, heads * kd), BF16),
        scratch_shapes=[pltpu.VMEM((hps, kd, kd), F32),
                        pltpu.VMEM((chunk + SUBLANE, wd), F32),
                        pltpu.VMEM((chunk + SUBLANE, wd), F32),
                        pltpu.VMEM((chunk + SUBLANE, wd), F32)],
        compiler_params=_params("arbitrary", "arbitrary", "arbitrary"),
        name="hg_core",
    )(qvg3, qvg3, qvg3, lf3, k3, norm_w, tri, masks, signs)
    return out.reshape(batch * seq, heads * kd)


def _layer_norm(x, g, b):
    mu = jnp.mean(x, axis=-1, keepdims=True)
    xc = x - mu
    var = jnp.mean(xc * xc, axis=-1, keepdims=True)
    return xc * lax.rsqrt(var + LN_EPS) * g + b


def _first_max(vals):
    best, idx = vals[0], jnp.zeros(vals[0].shape, I32)
    for i in range(1, len(vals)):
        better = vals[i] > best
        best = jnp.where(better, vals[i], best)
        idx = jnp.where(better, i, idx)
    return best, idx


def _top2(vals):
    top1, i1 = _first_max(vals)
    rest = [jnp.where(i1 == i, -jnp.inf, vals[i]) for i in range(len(vals))]
    top2, i2 = _first_max(rest)
    return top1, i1, top2, i2


def _route(logit_rows, n_groups):
    n_exp = len(logit_rows)
    per = n_exp // n_groups
    mx = functools.reduce(jnp.maximum, logit_rows)
    ex = [jnp.exp(l - mx) for l in logit_rows]
    den = functools.reduce(lambda a, b: a + b, ex)
    p = [e / den for e in ex]
    scores = []
    for g in range(n_groups):
        top1, _, top2, _ = _top2(p[g * per:(g + 1) * per])
        scores.append(top1 + top2)
    _, g_sel = _first_max(scores)
    in_group = []
    for i in range(per):
        val = p[i]
        for g in range(1, n_groups):
            val = jnp.where(g_sel == g, p[g * per + i], val)
        in_group.append(val)
    p1, i1, p2, i2 = _top2(in_group)
    tot = p1 + p2
    return g_sel * per + i1, g_sel * per + i2, p1 / tot, p2 / tot


def _ln_route_body(mix_ref, h_ref, g_ref, b_ref, rwh_ref, rwl_ref, rb_ref, o_ref, eid_ref, wt_ref,
                   *, alpha, n_exp, n_groups):
    y = _layer_norm(alpha * h_ref[...].astype(F32) + mix_ref[...], g_ref[...], b_ref[...])
    o_ref[...] = _pack_pairs(y)
    y_hi, y_lo = _split_bf16(y)
    logits = _dot(y_hi, rwh_ref[...]) + (_dot(y_hi, rwl_ref[...]) + _dot(y_lo, rwh_ref[...]))
    lt = logits.T + rb_ref[...]
    e1, e2, w1, w2 = _route([lt[e:e + 1, :] for e in range(n_exp)], n_groups)
    eid_ref[0:1, :] = e1
    eid_ref[1:2, :] = e2
    wt_ref[0:1, :] = w1
    wt_ref[1:2, :] = w2


def _ln_route(mix, h, g, b, router_w, router_b, *, alpha, tr):
    n, d = mix.shape
    n_exp = router_w.shape[1]
    tr = min(tr, n)
    assert n % tr == 0 and n_exp <= LANE and n_exp % N_GROUPS == 0 and d % (2 * LANE) == 0
    rw = jnp.zeros((d, LANE), F32).at[:, :n_exp].set(router_w.astype(F32))
    rw_hi = rw.astype(BF16)
    rw_lo = (rw - rw_hi.astype(F32)).astype(BF16)
    rb = jnp.zeros((LANE, 1), F32).at[:n_exp, 0].set(router_b.astype(F32))
    row = lambda: pl.BlockSpec((tr, d), lambda i: (i, 0))
    full = lambda shape: pl.BlockSpec(shape, lambda i: (0, 0))
    return pl.pallas_call(
        functools.partial(_ln_route_body, alpha=alpha, n_exp=n_exp, n_groups=N_GROUPS),
        grid=(n // tr,),
        in_specs=[row(), row(), full((1, d)), full((1, d)), full((d, LANE)), full((d, LANE)),
                  full((LANE, 1))],
        out_specs=[pl.BlockSpec((tr, d // 2), lambda i: (i, 0)),
                   pl.BlockSpec((2, tr), lambda i: (0, i)), pl.BlockSpec((2, tr), lambda i: (0, i))],
        out_shape=[jax.ShapeDtypeStruct((n, d // 2), U32), jax.ShapeDtypeStruct((2, n), I32),
                   jax.ShapeDtypeStruct((2, n), F32)],
        compiler_params=_params("arbitrary"),
        name="ln_route",
    )(mix, h, g, b, rw_hi, rw_lo, rb)


def _start_row_gather(src_hbm, idx_ref, dst, sem, n_rows):
    def pair(r2, c):
        for k in range(2):
            r = 2 * r2 + k
            pltpu.make_async_copy(src_hbm.at[pl.ds(idx_ref[0, 0, r], 1)], dst.at[pl.ds(r, 1)],
                                  sem).start(priority=k)
        return c

    lax.fori_loop(0, n_rows // 2, pair, 0)


def _wait_row_gather(src_hbm, dst, sem, n_rows):
    def wait(r, c):
        pltpu.make_async_copy(src_hbm.at[pl.ds(0, 1)], dst.at[pl.ds(r, 1)], sem).wait()
        return c

    lax.fori_loop(0, n_rows, wait, 0)


def _pipelined_gather(src_hbm, idx_ref, idx_next_ref, buf, sem, n_rows):
    i = pl.program_id(0)
    slot = lax.rem(i, 2)

    @pl.when(i == 0)
    def _():
        _start_row_gather(src_hbm, idx_ref, buf.at[0], sem.at[0], n_rows)

    @pl.when(i + 1 < pl.num_programs(0))
    def _():
        _start_row_gather(src_hbm, idx_next_ref, buf.at[1 - slot], sem.at[1 - slot], n_rows)

    _wait_row_gather(src_hbm, buf.at[slot], sem.at[slot], n_rows)
    return slot


def _idx_specs(tiles, width):
    return [pl.BlockSpec((1, 1, width), lambda i: (i, 0, 0), memory_space=pltpu.SMEM),
            pl.BlockSpec((1, 1, width), lambda i: (jnp.minimum(i + 1, tiles - 1), 0, 0),
                         memory_space=pltpu.SMEM)]


def _dispatch_body(src_ref, src_next_ref, h_hbm, xs_ref, buf, sem, *, td):
    slot = _pipelined_gather(h_hbm, src_ref, src_next_ref, buf, sem, td)
    xs_ref[...] = _unpack_pairs(buf[slot]).astype(xs_ref.dtype)


def _dispatch(h_packed, src_token, *, td):
    n, dh = h_packed.shape
    p = src_token.shape[0]
    td = min(td, p)
    assert p % td == 0 and td % 2 == 0
    src_tiles = src_token.reshape(p // td, 1, td)
    return pl.pallas_call(
        functools.partial(_dispatch_body, td=td),
        grid=(p // td,),
        in_specs=_idx_specs(p // td, td) + [pl.BlockSpec(memory_space=pl.ANY)],
        out_specs=pl.BlockSpec((td, 2 * dh), lambda i: (i, 0)),
        out_shape=jax.ShapeDtypeStruct((p, 2 * dh), BF16),
        scratch_shapes=[pltpu.VMEM((2, td, dh), U32), pltpu.SemaphoreType.DMA((2,))],
        compiler_params=_params("arbitrary"),
        name="moe_dispatch",
    )(src_tiles, src_tiles, h_packed)


def _moe_up_body(te_ref, nu_ref, x_ref, wg_ref, wu_ref, o_ref):
    @pl.when(pl.program_id(1) < nu_ref[0])
    def _():
        x = x_ref[...]
        o_ref[...] = (_silu(_dot(x, wg_ref[...])) * _dot(x, wu_ref[...])).astype(o_ref.dtype)

    @pl.when(pl.program_id(1) >= nu_ref[0])
    def _():
        o_ref[...] = jnp.zeros_like(o_ref)


def _moe_up(xs, w_in, layer, tile_expert, n_used, *, tm, tn):
    p, d = xs.shape
    ff = w_in.shape[3] // 2
    tn = min(tn, ff)
    assert p % tm == 0 and ff % tn == 0
    nj = ff // tn
    return pl.pallas_call(
        _moe_up_body,
        grid_spec=pltpu.PrefetchScalarGridSpec(
            num_scalar_prefetch=2,
            grid=(nj, p // tm),
            in_specs=[pl.BlockSpec((tm, d), lambda j, i, te, nu: (i, 0)),
                      pl.BlockSpec((None, None, d, tn), lambda j, i, te, nu: (layer, te[i], 0, j)),
                      pl.BlockSpec((None, None, d, tn), lambda j, i, te, nu: (layer, te[i], 0, nj + j))],
            out_specs=pl.BlockSpec((tm, tn), lambda j, i, te, nu: (i, j))),
        out_shape=jax.ShapeDtypeStruct((p, ff), BF16),
        compiler_params=_params("arbitrary", "arbitrary"),
        name="moe_up",
    )(tile_expert, n_used, xs, w_in, w_in)


def _moe_down_body(te_ref, nu_ref, x_ref, w_ref, o_ref):
    @pl.when(pl.program_id(0) < nu_ref[0])
    def _():
        o_ref[...] = _pack_pairs(_dot(x_ref[...], w_ref[...]))

    @pl.when(pl.program_id(0) >= nu_ref[0])
    def _():
        o_ref[...] = jnp.zeros_like(o_ref)


def _moe_down(h1, w_out, layer, tile_expert, n_used, *, tm):
    p, ff = h1.shape
    d = w_out.shape[3]
    assert p % tm == 0
    return pl.pallas_call(
        _moe_down_body,
        grid_spec=pltpu.PrefetchScalarGridSpec(
            num_scalar_prefetch=2,
            grid=(p // tm,),
            in_specs=[pl.BlockSpec((tm, ff), lambda i, te, nu: (i, 0)),
                      pl.BlockSpec((None, None, ff, d), lambda i, te, nu: (layer, te[i], 0, 0))],
            out_specs=pl.BlockSpec((tm, d // 2), lambda i, te, nu: (i, 0))),
        out_shape=jax.ShapeDtypeStruct((p, d // 2), U32),
        compiler_params=_params("arbitrary"),
        name="moe_down",
    )(tile_expert, n_used, h1, w_out)


def _combine_ln_body(pos_ref, pos_next_ref, ys_hbm, h_ref, wt_ref, g_ref, b_ref, o_ref, buf, sem,
                     *, tc, alpha):
    slot = _pipelined_gather(ys_hbm, pos_ref, pos_next_ref, buf, sem, 2 * tc)
    wt = wt_ref[...]
    rows = buf[slot]
    ffn = wt[:, 0:1] * _unpack_pairs(rows[0:tc]) + wt[:, 1:2] * _unpack_pairs(rows[tc:2 * tc])
    x = alpha * _unpack_pairs(h_ref[...]) + ffn
    o_ref[...] = _layer_norm(x, g_ref[...], b_ref[...]).astype(o_ref.dtype)


def _combine_ln(ys, pos, h_packed, wts, g, b, *, alpha, out_dtype, tc):
    n, dh = h_packed.shape
    d = 2 * dh
    tc = min(tc, n)
    assert n % tc == 0
    pos_tiles = jnp.concatenate([pos[0].reshape(n // tc, 1, tc), pos[1].reshape(n // tc, 1, tc)], axis=2)
    return pl.pallas_call(
        functools.partial(_combine_ln_body, tc=tc, alpha=alpha),
        grid=(n // tc,),
        in_specs=_idx_specs(n // tc, 2 * tc) + [
            pl.BlockSpec(memory_space=pl.ANY),
            pl.BlockSpec((tc, dh), lambda i: (i, 0)),
            pl.BlockSpec((tc, 2), lambda i: (i, 0)),
            pl.BlockSpec((1, d), lambda i: (0, 0)),
            pl.BlockSpec((1, d), lambda i: (0, 0))],
        out_specs=pl.BlockSpec((tc, d), lambda i: (i, 0)),
        out_shape=jax.ShapeDtypeStruct((n, d), out_dtype),
        scratch_shapes=[pltpu.VMEM((2, 2 * tc, dh), U32), pltpu.SemaphoreType.DMA((2,))],
        compiler_params=_params("arbitrary"),
        name="moe_combine_ln",
    )(pos_tiles, pos_tiles, ys, h_packed, wts.T, g, b)


def _moe_plan(eid, n_exp, tm):
    n = eid.shape[1]
    flat = eid.reshape(-1)
    onehot = (flat[:, None] == jnp.arange(n_exp, dtype=I32)[None, :]).astype(I32)
    csum = jnp.cumsum(onehot, axis=0)
    cnt = csum[-1]
    tiles = (cnt + tm - 1) // tm
    tile_end = jnp.cumsum(tiles)
    tile_start = tile_end - tiles
    grp_start = jnp.cumsum(cnt) - cnt
    pos = jnp.sum(onehot * (csum - 1 + (tile_start * tm)[None, :]), axis=1).reshape(2, n)

    n_tiles = (2 * n) // tm + n_exp
    n_used = tile_end[-1]
    tile_id = jnp.arange(n_tiles, dtype=I32)
    te = jnp.minimum(jnp.searchsorted(tile_end, tile_id, side="right"), n_exp - 1).astype(I32)
    te = jnp.where(tile_id < n_used, te, te[jnp.maximum(n_used - 1, 0)])

    order = jnp.argsort(flat, stable=True).astype(I32)
    base = grp_start[te] + (tile_id - tile_start[te]) * tm
    last = grp_start[te] + cnt[te] - 1
    idx = jnp.minimum(base[:, None] + jnp.arange(tm, dtype=I32)[None, :], last[:, None])
    src = order[jnp.clip(idx, 0, 2 * n - 1).reshape(-1)] % n
    return pos.astype(I32), src.astype(I32), te, n_used.astype(I32).reshape(1)


def _moe_ln(h_packed, eid, wts, w_in, w_out, layer, g, b, *, alpha, out_dtype, tm):
    n_exp = w_in.shape[1]
    pos, src, tile_expert, n_used = _moe_plan(eid, n_exp, tm)
    xs = _dispatch(h_packed, src, td=tm)
    h1 = _moe_up(xs, w_in, layer, tile_expert, n_used, tm=tm, tn=768)
    ys = _moe_down(h1, w_out, layer, tile_expert, n_used, tm=tm)
    return _combine_ln(ys, pos, h_packed, wts, g, b, alpha=alpha, out_dtype=out_dtype, tc=256)


def _pad_heads(a, nh, hd, hdp, axis):
    shape = a.shape
    a = a.reshape(shape[:axis] + (nh, hd) + shape[axis + 1:])
    pad = [(0, 0)] * a.ndim
    pad[axis + 1] = (0, hdp - hd)
    a = jnp.pad(a, pad)
    return a.reshape(shape[:axis] + (nh * hdp,) + shape[axis + 1:])


def kernel(x, rg_w_in, rg_conv_w, rg_conv_b, rg_w_ra, rg_b_ra, rg_w_ix, rg_b_ix, rg_lambda, rg_w_out,
           hg_w_in, hg_norm_w, hg_w_out, hg_lb_logits, router_w, router_b, moe_w_in, moe_w_out, ln_g, ln_b):
    batch, seq, d = x.shape
    n = batch * seq
    depth = ln_g.shape[0]
    alpha = (2.0 * depth) ** 0.25
    nh, hd = rg_w_ra.shape[1], rg_w_ra.shape[2]
    hdp = _round_up(hd, LANE)
    hv = hg_w_out.shape[1]
    hk = (hg_w_in.shape[2] - 2 * hv) // 2
    heads = hv // hg_norm_w.shape[1]
    assert hk == hv, "HGRN2 key and value widths must match"
    moe_tm = min(512, n)
    moe_w_in_bf = moe_w_in.astype(BF16)
    moe_w_out_bf = moe_w_out.astype(BF16)

    h = x.reshape(n, d)
    h_bf = h.astype(BF16)
    for layer in range(depth):
        j = layer // 2
        if layer % 2 == 0:
            w_in = _pad_heads(rg_w_in[j].astype(BF16).reshape(d, 2, nh * hd), nh, hd, hdp, 2)
            proj = _matmul(h_bf, w_in.reshape(d, 2 * nh * hdp), acts=(jax.nn.gelu, _identity),
                           col_starts=(0, nh * hdp), width=nh * hdp, out_dtype=BF16,
                           tm=1024, tn=1024, name="rg_in_proj")
            pad_vec = lambda v: _pad_heads(v.reshape(1, nh * hd).astype(F32), nh, hd, hdp, 1)
            pad_mat = lambda w: jnp.pad(w, ((0, 0), (0, hdp - hd), (0, hdp - hd))).astype(BF16)
            y = _rg_core(proj, _pad_heads(rg_conv_w[j].astype(F32), nh, hd, hdp, 1), pad_vec(rg_conv_b[j]),
                         pad_mat(rg_w_ra[j]), pad_vec(rg_b_ra[j]), pad_mat(rg_w_ix[j]), pad_vec(rg_b_ix[j]),
                         pad_vec(rg_lambda[j]), batch=batch, seq=seq, tt=512)
            w_out = _pad_heads(rg_w_out[j].astype(BF16), nh, hd, hdp, 0)
            mix = _matmul(y, w_out, acts=(_identity,), col_starts=(0,), width=d, out_dtype=F32,
                          tm=512, tn=1024, name="rg_out_proj")
        else:
            w = hg_w_in[j].astype(BF16)
            qvg = _matmul(h_bf, w, acts=(_silu, _identity, _silu), col_starts=(0, 2 * hk, 2 * hk + hv),
                          width=hk, out_dtype=BF16, tm=1024, tn=1024, name="hg_in_proj")
            log_f, key = _hg_gate_proj(h_bf, w, hg_lb_logits.astype(F32), col_start=hk, width=hk,
                                       layer=layer, tm=1024, tn=1024)
            o = _hg_core(qvg, log_f, key, hg_norm_w[j].reshape(1, -1).astype(F32),
                         batch=batch, seq=seq, heads=heads, tt=1024)
            mix = _matmul(o, hg_w_out[j].astype(BF16), acts=(_identity,), col_starts=(0,), width=d,
                          out_dtype=F32, tm=1024, tn=1024, name="hg_out_proj")
        g1, b1 = ln_g[layer, 0].reshape(1, d).astype(F32), ln_b[layer, 0].reshape(1, d).astype(F32)
        g2, b2 = ln_g[layer, 1].reshape(1, d).astype(F32), ln_b[layer, 1].reshape(1, d).astype(F32)
        h_packed, eid, wts = _ln_route(mix, h, g1, b1, router_w, router_b, alpha=alpha, tr=256)
        last = layer == depth - 1
        h = _moe_ln(h_packed, eid, wts, moe_w_in_bf, moe_w_out_bf, layer, g2, b2, alpha=alpha,
                    out_dtype=x.dtype if last else BF16, tm=moe_tm)
        h_bf = h
    return h.reshape(batch, seq, d)
```

```python
import functools

import numpy as np
import jax
import jax.numpy as jnp
from jax import lax
from jax.experimental import pallas as pl
from jax.experimental.pallas import tpu as pltpu

F32 = jnp.float32
BF16 = jnp.bfloat16
I32 = jnp.int32
U32 = jnp.uint32

LANE = 128
SUBLANE = 8
VMEM_LIMIT_BYTES = 56 * 2**20

RG_C = 8.0
N_GROUPS = 4
LN_EPS = 1e-5
RMS_EPS = 1e-6
HG_CHUNK = 128
HG_BAND = 4
HG_HEADS_PER_STEP = 2
LOG2_E = 1.4426950408889634


def _params(*sem):
    return pltpu.CompilerParams(dimension_semantics=sem, vmem_limit_bytes=VMEM_LIMIT_BYTES)


def _round_up(x, m):
    return (x + m - 1) // m * m


def _silu(x):
    return x * jax.nn.sigmoid(x)


def _identity(x):
    return x


def _softplus(x):
    return jnp.maximum(x, 0.0) + jnp.log1p(jnp.exp(-jnp.abs(x)))


def _dot(a, b):
    return jnp.dot(a, b, preferred_element_type=F32)


def _dot_nt(a, b):
    return lax.dot_general(a, b, (((1,), (1,)), ((), ())), preferred_element_type=F32)


def _dot_tn(a, b):
    return lax.dot_general(a, b, (((0,), (0,)), ((), ())), preferred_element_type=F32)


def _split_bf16(x):
    hi = x.astype(BF16)
    lo = (x - hi.astype(F32)).astype(BF16)
    return hi, lo


def _pack_pairs(y):
    half = y.shape[1] // 2
    bits = lax.bitcast_convert_type(y.astype(BF16).astype(F32), U32)
    return bits[:, :half] | (bits[:, half:] >> 16)


def _unpack_pairs(p):
    hi = lax.bitcast_convert_type(p & jnp.uint32(0xFFFF0000), F32)
    lo = lax.bitcast_convert_type(p << 16, F32)
    return jnp.concatenate([hi, lo], axis=1)


def _mm_body(x_ref, w_ref, o_ref, *, acts, tiles_per_region):
    acc = _dot(x_ref[...], w_ref[...])
    if len(acts) == 1:
        o_ref[...] = acts[0](acc).astype(o_ref.dtype)
        return
    region = pl.program_id(1) // tiles_per_region
    for r, act in enumerate(acts):
        @pl.when(region == r)
        def _(act=act):
            o_ref[...] = act(acc).astype(o_ref.dtype)


def _matmul(x, w, *, acts, col_starts, width, out_dtype, tm, tn, name):
    m, k = x.shape
    tm, tn = min(tm, m), min(tn, width)
    assert m % tm == 0 and width % tn == 0 and all(c % tn == 0 for c in col_starts)
    per = width // tn
    starts = [c // tn for c in col_starts]

    def w_index(i, j):
        col = starts[0] + j
        for r in range(1, len(starts)):
            col = jnp.where(j >= r * per, starts[r] + (j - r * per), col)
        return (0, col)

    return pl.pallas_call(
        functools.partial(_mm_body, acts=acts, tiles_per_region=per),
        grid=(m // tm, per * len(acts)),
        in_specs=[pl.BlockSpec((tm, k), lambda i, j: (i, 0)),
                  pl.BlockSpec((k, tn), w_index)],
        out_specs=pl.BlockSpec((tm, tn), lambda i, j: (i, j)),
        out_shape=jax.ShapeDtypeStruct((m, width * len(acts)), out_dtype),
        compiler_params=_params("arbitrary", "arbitrary"),
        name=name,
    )(x, w)


def _hg_gate_body(x_ref, w_ref, lbl_ref, lf_ref, k_ref, *, layer):
    z = _dot(x_ref[...], w_ref[...])
    logits = lbl_ref[...]
    mx = jnp.max(logits, axis=0, keepdims=True)
    ex = jnp.exp(logits - mx)
    probs = ex / jnp.sum(ex, axis=0, keepdims=True)
    lb = jnp.zeros_like(mx)
    for l in range(1, layer + 1):
        lb = lb + probs[l:l + 1, :]
    t = jnp.exp(-jnp.abs(z))
    log_sig = jnp.minimum(z, 0.0) - jnp.log1p(t)
    la = jnp.log(lb)
    lc = jnp.log1p(-lb) + log_sig
    lf_ref[...] = LOG2_E * (jnp.maximum(la, lc) + jnp.log1p(jnp.exp(-jnp.abs(la - lc))))
    sig_neg = jnp.where(z >= 0.0, t, 1.0) / (1.0 + t)
    k_ref[...] = ((1.0 - lb) * sig_neg).astype(k_ref.dtype)


def _hg_gate_proj(x, w, lb_logits, *, col_start, width, layer, tm, tn):
    m, k = x.shape
    depth = lb_logits.shape[0]
    tm, tn = min(tm, m), min(tn, width)
    assert m % tm == 0 and width % tn == 0 and col_start % tn == 0
    off = col_start // tn
    return pl.pallas_call(
        functools.partial(_hg_gate_body, layer=layer),
        grid=(m // tm, width // tn),
        in_specs=[pl.BlockSpec((tm, k), lambda i, j: (i, 0)),
                  pl.BlockSpec((k, tn), lambda i, j: (0, off + j)),
                  pl.BlockSpec((depth, tn), lambda i, j: (0, j))],
        out_specs=[pl.BlockSpec((tm, tn), lambda i, j: (i, j)),
                   pl.BlockSpec((tm, tn), lambda i, j: (i, j))],
        out_shape=[jax.ShapeDtypeStruct((m, width), F32), jax.ShapeDtypeStruct((m, width), BF16)],
        compiler_params=_params("arbitrary", "arbitrary"),
        name="hg_gate_proj",
    )(x, w, lb_logits)


def _rg_body(gate_ref, xr_ref, cw_ref, cb_ref, wra_ref, bra_ref, wix_ref, bix_ref, lam_ref,
             y_ref, xpad, a_scr, u_scr, hcar, *, tt, conv_w):
    @pl.when(pl.program_id(2) == 0)
    def _():
        xpad[0:SUBLANE, :] = jnp.zeros((SUBLANE, xpad.shape[1]), F32)
        hcar[...] = jnp.zeros_like(hcar)

    x = xr_ref[...].astype(F32)
    xpad[SUBLANE:SUBLANE + tt, :] = x
    xc = cb_ref[...] + cw_ref[conv_w - 1:conv_w, :] * x
    for j in range(conv_w - 1):
        xc = xc + cw_ref[j:j + 1, :] * xpad[pl.ds(SUBLANE - (conv_w - 1) + j, tt), :]
    xpad[0:SUBLANE, :] = x[tt - SUBLANE:tt, :]

    xcb = xc.astype(BF16)
    r = jax.nn.sigmoid(_dot(xcb, wra_ref[...]) + bra_ref[...])
    ig = jax.nn.sigmoid(_dot(xcb, wix_ref[...]) + bix_ref[...])
    log_a = (-RG_C * _softplus(-lam_ref[...])) * r
    a = jnp.exp(log_a)
    a_scr[...] = a
    u_scr[...] = jnp.sqrt(-jnp.tanh(log_a) * (a * a + 1.0)) * (ig * xc)

    row = lax.broadcasted_iota(I32, (SUBLANE, xpad.shape[1]), 0)

    def block(i, hprev):
        r0 = pl.multiple_of(i * SUBLANE, SUBLANE)
        a = a_scr[pl.ds(r0, SUBLANE), :]
        u = u_scr[pl.ds(r0, SUBLANE), :]
        for s in (1, 2, 4):
            keep = row >= s
            u = jnp.where(keep, a * pltpu.roll(u, s, axis=0) + u, u)
            a = jnp.where(keep, a * pltpu.roll(a, s, axis=0), a)
        h = a * hprev + u
        u_scr[pl.ds(r0, SUBLANE), :] = h
        return jnp.broadcast_to(h[SUBLANE - 1:SUBLANE, :], h.shape)

    hcar[...] = lax.fori_loop(0, tt // SUBLANE, block, hcar[...])
    y_ref[...] = (u_scr[...] * gate_ref[...].astype(F32)).astype(y_ref.dtype)


def _rg_core(proj, conv_w, conv_b, w_ra, b_ra, w_ix, b_ix, lam, *, batch, seq, tt):
    nh, hd = w_ra.shape[0], w_ra.shape[1]
    cw = conv_w.shape[0]
    tt = min(tt, seq)
    assert seq % tt == 0 and tt % SUBLANE == 0 and cw - 1 <= SUBLANE
    proj3 = proj.reshape(batch, seq, 2 * nh * hd)
    vec = lambda: pl.BlockSpec((1, hd), lambda b, h, t: (0, h))
    y = pl.pallas_call(
        functools.partial(_rg_body, tt=tt, conv_w=cw),
        grid=(batch, nh, seq // tt),
        in_specs=[pl.BlockSpec((None, tt, hd), lambda b, h, t: (b, t, h)),
                  pl.BlockSpec((None, tt, hd), lambda b, h, t: (b, t, nh + h)),
                  pl.BlockSpec((cw, hd), lambda b, h, t: (0, h)),
                  vec(),
                  pl.BlockSpec((None, hd, hd), lambda b, h, t: (h, 0, 0)),
                  vec(),
                  pl.BlockSpec((None, hd, hd), lambda b, h, t: (h, 0, 0)),
                  vec(),
                  vec()],
        out_specs=pl.BlockSpec((None, tt, hd), lambda b, h, t: (b, t, h)),
        out_shape=jax.ShapeDtypeStruct((batch, seq, nh * hd), BF16),
        scratch_shapes=[pltpu.VMEM((tt + SUBLANE, hd), F32),
                        pltpu.VMEM((tt, hd), F32),
                        pltpu.VMEM((tt, hd), F32),
                        pltpu.VMEM((SUBLANE, hd), F32)],
        compiler_params=_params("arbitrary", "arbitrary", "arbitrary"),
        name="rg_core",
    )(proj3, proj3, conv_w, conv_b, w_ra, b_ra, w_ix, b_ix, lam)
    return y.reshape(batch * seq, nh * hd)


def _hg_tables(chunk, band, width):
    t = np.arange(chunk)[:, None]
    s = np.arange(chunk)[None, :]
    masks, signs = [], []
    half = chunk // 2
    while half >= band:
        same_parent = (t // (2 * half)) == (s // (2 * half))
        split = ((t // half) % 2 == 1) & ((s // half) % 2 == 0)
        masks.append((same_parent & split & (t - s >= band)).astype(np.float32))
        signs.append(np.broadcast_to(np.where((t // half) % 2 == 1, 1.0, -1.0), (chunk, width)).astype(np.float32))
        half //= 2
    return np.stack(masks), np.stack(signs)


def _hg_body(q_ref, v_ref, g_ref, lf_ref, k_ref, nw_ref, tri_ref, mask_ref, sign_ref, o_ref,
             st_ref, kpad, bpad, vpad, *, tt, chunk, band, heads):
    kd = LANE

    @pl.when(pl.program_id(2) == 0)
    def _():
        st_ref[...] = jnp.zeros_like(st_ref)

    zpad = jnp.zeros((SUBLANE, heads * kd), F32)
    kpad[0:SUBLANE, :] = zpad
    bpad[0:SUBLANE, :] = zpad
    vpad[0:SUBLANE, :] = zpad
    tri = tri_ref[...]
    row = lax.broadcasted_iota(I32, (chunk, 1), 0)

    def one_chunk(c, carry):
        r0 = pl.multiple_of(c * chunk, chunk)
        q2 = q_ref[pl.ds(r0, chunk), :].astype(F32)
        k2 = k_ref[pl.ds(r0, chunk), :].astype(F32)
        vb2 = v_ref[pl.ds(r0, chunk), :]
        g2 = g_ref[pl.ds(r0, chunk), :].astype(F32)
        lf_hi, lf_lo = _split_bf16(lf_ref[pl.ds(r0, chunk), :])
        b2 = _dot(tri, lf_hi) + _dot(tri, lf_lo)
        kpad[SUBLANE:SUBLANE + chunk, :] = k2
        bpad[SUBLANE:SUBLANE + chunk, :] = b2
        vpad[SUBLANE:SUBLANE + chunk, :] = vb2.astype(F32)

        outs = []
        for h in range(heads):
            sl = slice(h * kd, (h + 1) * kd)
            q, k, vb, b = q2[:, sl], k2[:, sl], vb2[:, sl], b2[:, sl]
            b_last = b[chunk - 1:chunk, :]

            st = st_ref[h]
            o = _dot_nt((q * jnp.exp2(b)).astype(BF16), st.astype(BF16))
            k_dec = (k * jnp.exp2(b_last - b)).astype(BF16)
            st_ref[h] = st * jnp.exp2(b_last) + _dot_tn(vb, k_dec)

            scores = jnp.zeros((chunk, chunk), BF16)
            half, level = chunk // 2, 0
            while half >= band:
                if half >= SUBLANE:
                    rows = []
                    for p0 in range(0, chunk, 2 * half):
                        r = b[p0 + half - 1:p0 + half, :]
                        lo_, mid_, hi_ = p0, p0 + half, p0 + 2 * half
                        rows.append(k[lo_:mid_] * jnp.exp2(r - b[lo_:mid_]))
                        rows.append(q[mid_:hi_] * jnp.exp2(b[mid_:hi_] - r))
                    both = jnp.concatenate(rows, axis=0).astype(BF16)
                else:
                    b3 = b.reshape(chunk // (2 * half), 2 * half, kd)
                    ref = jnp.broadcast_to(b3[:, half - 1:half, :], b3.shape).reshape(chunk, kd)
                    sign = sign_ref[level]
                    e = jnp.exp2((b - ref) * sign)
                    both = (jnp.where(sign > 0.0, q, k) * e).astype(BF16)
                scores = scores + mask_ref[level] * _dot_nt(both, both).astype(BF16)
                half //= 2
                level += 1
            o = o + _dot(scores, vb)

            o = o + jnp.sum(q * k, axis=-1, keepdims=True) * vb.astype(F32)
            for d in range(1, band):
                kd_ = kpad[pl.ds(SUBLANE - d, chunk), sl]
                bd_ = bpad[pl.ds(SUBLANE - d, chunk), sl]
                vd_ = vpad[pl.ds(SUBLANE - d, chunk), sl]
                w = q * kd_ * jnp.exp2(jnp.minimum(b - bd_, 0.0))
                sc = jnp.sum(w, axis=-1, keepdims=True)
                o = o + jnp.where(row >= d, sc, 0.0) * vd_

            ms = jnp.mean(o * o, axis=-1, keepdims=True)
            outs.append(o * lax.rsqrt(ms + RMS_EPS) * nw_ref[...] * g2[:, sl])
        y = outs[0] if heads == 1 else jnp.concatenate(outs, axis=1)
        o_ref[pl.ds(r0, chunk), :] = y.astype(o_ref.dtype)
        return carry

    lax.fori_loop(0, tt // chunk, one_chunk, 0, unroll=True)


def _hg_core(qvg, log_f, key, norm_w, *, batch, seq, heads, tt):
    kd = norm_w.shape[-1]
    assert kd == LANE, "head key/value width must equal the lane width"
    hps = HG_HEADS_PER_STEP if heads % HG_HEADS_PER_STEP == 0 else 1
    tt = min(tt, seq)
    chunk = min(HG_CHUNK, tt)
    assert seq % tt == 0 and tt % chunk == 0 and chunk % (2 * SUBLANE) == 0 and chunk & (chunk - 1) == 0
    masks_np, signs_np = _hg_tables(chunk, HG_BAND, kd)
    masks = jnp.asarray(masks_np, dtype=BF16)
    signs = jnp.asarray(signs_np)
    tri = jnp.asarray(np.tril(np.ones((chunk, chunk), np.float32)), dtype=BF16)
    wd = hps * kd
    groups = heads // hps
    qvg3 = qvg.reshape(batch, seq, 3 * heads * kd)
    lf3 = log_f.reshape(batch, seq, heads * kd)
    k3 = key.reshape(batch, seq, heads * kd)
    blk = lambda off: pl.BlockSpec((None, tt, wd), lambda b, h, t, off=off: (b, t, off + h))
    const = lambda a: pl.BlockSpec(a.shape, lambda b, h, t: (0,) * a.ndim)
    out = pl.pallas_call(
        functools.partial(_hg_body, tt=tt, chunk=chunk, band=HG_BAND, heads=hps),
        grid=(batch, groups, seq // tt),
        in_specs=[blk(0), blk(groups), blk(2 * groups), blk(0), blk(0),
                  const(norm_w), const(tri), const(masks), const(signs)],
        out_specs=blk(0),
        out_shape=jax.ShapeDtypeStruct((batch, seq, heads * kd), BF16),
        scratch_shapes=[pltpu.VMEM((hps, kd, kd), F32),
                        pltpu.VMEM((chunk + SUBLANE, wd), F32),
                        pltpu.VMEM((chunk + SUBLANE, wd), F32),
                        pltpu.VMEM((chunk + SUBLANE, wd), F32)],
        compiler_params=_params("arbitrary", "arbitrary", "arbitrary"),
        name="hg_core",
    )(qvg3, qvg3, qvg3, lf3, k3, norm_w, tri, masks, signs)
    return out.reshape(batch * seq, heads * kd)


def _layer_norm(x, g, b):
    mu = jnp.mean(x, axis=-1, keepdims=True)
    xc = x - mu
    var = jnp.mean(xc * xc, axis=-1, keepdims=True)
    return xc * lax.rsqrt(var + LN_EPS) * g + b


def _first_max(vals):
    best, idx = vals[0], jnp.zeros(vals[0].shape, I32)
    for i in range(1, len(vals)):
        better = vals[i] > best
        best = jnp.where(better, vals[i], best)
        idx = jnp.where(better, i, idx)
    return best, idx


def _top2(vals):
    top1, i1 = _first_max(vals)
    rest = [jnp.where(i1 == i, -jnp.inf, vals[i]) for i in range(len(vals))]
    top2, i2 = _first_max(rest)
    return top1, i1, top2, i2


def _route(logit_rows, n_groups):
    n_exp = len(logit_rows)
    per = n_exp // n_groups
    mx = functools.reduce(jnp.maximum, logit_rows)
    ex = [jnp.exp(l - mx) for l in logit_rows]
    den = functools.reduce(lambda a, b: a + b, ex)
    p = [e / den for e in ex]
    scores = []
    for g in range(n_groups):
        top1, _, top2, _ = _top2(p[g * per:(g + 1) * per])
        scores.append(top1 + top2)
    _, g_sel = _first_max(scores)
    in_group = []
    for i in range(per):
        val = p[i]
        for g in range(1, n_groups):
            val = jnp.where(g_sel == g, p[g * per + i], val)
        in_group.append(val)
    p1, i1, p2, i2 = _top2(in_group)
    tot = p1 + p2
    return g_sel * per + i1, g_sel * per + i2, p1 / tot, p2 / tot


def _ln_route_body(mix_ref, h_ref, g_ref, b_ref, rwh_ref, rwl_ref, rb_ref, o_ref, eid_ref, wt_ref,
                   *, alpha, n_exp, n_groups):
    y = _layer_norm(alpha * h_ref[...].astype(F32) + mix_ref[...], g_ref[...], b_ref[...])
    o_ref[...] = _pack_pairs(y)
    y_hi, y_lo = _split_bf16(y)
    logits = _dot(y_hi, rwh_ref[...]) + (_dot(y_hi, rwl_ref[...]) + _dot(y_lo, rwh_ref[...]))
    lt = logits.T + rb_ref[...]
    e1, e2, w1, w2 = _route([lt[e:e + 1, :] for e in range(n_exp)], n_groups)
    eid_ref[0:1, :] = e1
    eid_ref[1:2, :] = e2
    wt_ref[0:1, :] = w1
    wt_ref[1:2, :] = w2


def _ln_route(mix, h, g, b, router_w, router_b, *, alpha, tr):
    n, d = mix.shape
    n_exp = router_w.shape[1]
    tr = min(tr, n)
    assert n % tr == 0 and n_exp <= LANE and n_exp % N_GROUPS == 0 and d % (2 * LANE) == 0
    rw = jnp.zeros((d, LANE), F32).at[:, :n_exp].set(router_w.astype(F32))
    rw_hi = rw.astype(BF16)
    rw_lo = (rw - rw_hi.astype(F32)).astype(BF16)
    rb = jnp.zeros((LANE, 1), F32).at[:n_exp, 0].set(router_b.astype(F32))
    row = lambda: pl.BlockSpec((tr, d), lambda i: (i, 0))
    full = lambda shape: pl.BlockSpec(shape, lambda i: (0, 0))
    return pl.pallas_call(
        functools.partial(_ln_route_body, alpha=alpha, n_exp=n_exp, n_groups=N_GROUPS),
        grid=(n // tr,),
        in_specs=[row(), row(), full((1, d)), full((1, d)), full((d, LANE)), full((d, LANE)),
                  full((LANE, 1))],
        out_specs=[pl.BlockSpec((tr, d // 2), lambda i: (i, 0)),
                   pl.BlockSpec((2, tr), lambda i: (0, i)), pl.BlockSpec((2, tr), lambda i: (0, i))],
        out_shape=[jax.ShapeDtypeStruct((n, d // 2), U32), jax.ShapeDtypeStruct((2, n), I32),
                   jax.ShapeDtypeStruct((2, n), F32)],
        compiler_params=_params("arbitrary"),
        name="ln_route",
    )(mix, h, g, b, rw_hi, rw_lo, rb)


def _start_row_gather(src_hbm, idx_ref, dst, sem, n_rows):
    for r in range(n_rows):
        pltpu.make_async_copy(src_hbm.at[pl.ds(idx_ref[0, 0, r], 1)], dst.at[pl.ds(r, 1)], sem).start()


def _wait_row_gather(src_hbm, dst, sem, n_rows):
    for r in range(n_rows):
        pltpu.make_async_copy(src_hbm.at[pl.ds(0, 1)], dst.at[pl.ds(r, 1)], sem).wait()


def _pipelined_gather(src_hbm, idx_ref, idx_next_ref, buf, sem, n_rows, n_active):
    i = pl.program_id(0)
    slot = lax.rem(i, 2)

    @pl.when((i == 0) & (n_active > 0))
    def _():
        _start_row_gather(src_hbm, idx_ref, buf.at[0], sem.at[0], n_rows)

    @pl.when(i + 1 < n_active)
    def _():
        _start_row_gather(src_hbm, idx_next_ref, buf.at[1 - slot], sem.at[1 - slot], n_rows)

    @pl.when(i < n_active)
    def _():
        _wait_row_gather(src_hbm, buf.at[slot], sem.at[slot], n_rows)

    return slot


def _idx_specs(tiles, width):
    return [pl.BlockSpec((1, 1, width), lambda i, *_: (i, 0, 0), memory_space=pltpu.SMEM),
            pl.BlockSpec((1, 1, width), lambda i, *_: (jnp.minimum(i + 1, tiles - 1), 0, 0),
                         memory_space=pltpu.SMEM)]


def _load_expert_weights(w_hbm, layer, expert, w_bf, stage, wsem, *, cols):
    ff = w_bf.shape[2]
    n_chunks = (2 * ff) // cols

    def chunk_copy(c):
        return pltpu.make_async_copy(w_hbm.at[layer, expert, :, pl.ds(c * cols, cols)],
                                     stage.at[c % 2], wsem.at[c % 2])

    chunk_copy(0).start()
    for c in range(n_chunks):
        if c + 1 < n_chunks:
            chunk_copy(c + 1).start()
        chunk_copy(c).wait()
        half, col = divmod(c * cols, ff)
        w_bf[half, :, col:col + cols] = stage[c % 2].astype(BF16)


def _moe_up_body(te_ref, nu_ref, src_ref, src_next_ref, h_hbm, w_hbm, o_ref, buf, sem, w_bf, stage, wsem,
                 *, tm, layer, cols):
    i = pl.program_id(0)
    slot = _pipelined_gather(h_hbm, src_ref, src_next_ref, buf, sem, tm, nu_ref[0])
    expert = te_ref[i]

    @pl.when((i < nu_ref[0]) & ((i == 0) | (expert != te_ref[jnp.maximum(i - 1, 0)])))
    def _():
        _load_expert_weights(w_hbm, layer, expert, w_bf, stage, wsem, cols=cols)

    @pl.when(i < nu_ref[0])
    def _():
        x = _unpack_pairs(buf[slot]).astype(BF16)
        o_ref[...] = (_silu(_dot(x, w_bf[0])) * _dot(x, w_bf[1])).astype(o_ref.dtype)

    @pl.when(i >= nu_ref[0])
    def _():
        o_ref[...] = jnp.zeros_like(o_ref)


def _moe_up(h_packed, src_token, w_in, layer, tile_expert, n_used, *, tm):
    n, dh = h_packed.shape
    d = 2 * dh
    p = src_token.shape[0]
    ff = w_in.shape[3] // 2
    cols = min(LANE, ff)
    assert p % tm == 0 and ff % cols == 0
    tiles = p // tm
    src_tiles = src_token.reshape(tiles, 1, tm)
    return pl.pallas_call(
        functools.partial(_moe_up_body, tm=tm, layer=layer, cols=cols),
        grid_spec=pltpu.PrefetchScalarGridSpec(
            num_scalar_prefetch=2,
            grid=(tiles,),
            in_specs=_idx_specs(tiles, tm) + [pl.BlockSpec(memory_space=pl.ANY),
                                              pl.BlockSpec(memory_space=pl.ANY)],
            out_specs=pl.BlockSpec((tm, ff), lambda i, te, nu: (i, 0)),
            scratch_shapes=[pltpu.VMEM((2, tm, dh), U32), pltpu.SemaphoreType.DMA((2,)),
                            pltpu.VMEM((2, d, ff), BF16), pltpu.VMEM((2, d, cols), w_in.dtype),
                            pltpu.SemaphoreType.DMA((2,))]),
        out_shape=jax.ShapeDtypeStruct((p, ff), BF16),
        compiler_params=_params("arbitrary"),
        name="moe_up",
    )(tile_expert, n_used, src_tiles, src_tiles, h_packed, w_in)


def _moe_down_body(te_ref, nu_ref, x_ref, w_ref, o_ref):
    @pl.when(pl.program_id(0) < nu_ref[0])
    def _():
        o_ref[...] = _pack_pairs(_dot(x_ref[...], w_ref[...]))

    @pl.when(pl.program_id(0) >= nu_ref[0])
    def _():
        o_ref[...] = jnp.zeros_like(o_ref)


def _moe_down(h1, w_out, layer, tile_expert, n_used, *, tm):
    p, ff = h1.shape
    d = w_out.shape[3]
    assert p % tm == 0
    return pl.pallas_call(
        _moe_down_body,
        grid_spec=pltpu.PrefetchScalarGridSpec(
            num_scalar_prefetch=2,
            grid=(p // tm,),
            in_specs=[pl.BlockSpec((tm, ff), lambda i, te, nu: (i, 0)),
                      pl.BlockSpec((None, None, ff, d), lambda i, te, nu: (layer, te[i], 0, 0))],
            out_specs=pl.BlockSpec((tm, d // 2), lambda i, te, nu: (i, 0))),
        out_shape=jax.ShapeDtypeStruct((p, d // 2), U32),
        compiler_params=_params("arbitrary"),
        name="moe_down",
    )(tile_expert, n_used, h1, w_out)


def _combine_ln_body(pos_ref, pos_next_ref, ys_hbm, h_ref, wt_ref, g_ref, b_ref, o_ref, buf, sem,
                     *, tc, alpha):
    slot = _pipelined_gather(ys_hbm, pos_ref, pos_next_ref, buf, sem, 2 * tc, pl.num_programs(0))
    wt = wt_ref[...]
    rows = buf[slot]
    ffn = wt[:, 0:1] * _unpack_pairs(rows[0:tc]) + wt[:, 1:2] * _unpack_pairs(rows[tc:2 * tc])
    x = alpha * _unpack_pairs(h_ref[...]) + ffn
    o_ref[...] = _layer_norm(x, g_ref[...], b_ref[...]).astype(o_ref.dtype)


def _combine_ln(ys, pos, h_packed, wts, g, b, *, alpha, out_dtype, tc):
    n, dh = h_packed.shape
    d = 2 * dh
    tc = min(tc, n)
    assert n % tc == 0
    pos_tiles = jnp.concatenate([pos[0].reshape(n // tc, 1, tc), pos[1].reshape(n // tc, 1, tc)], axis=2)
    return pl.pallas_call(
        functools.partial(_combine_ln_body, tc=tc, alpha=alpha),
        grid=(n // tc,),
        in_specs=_idx_specs(n // tc, 2 * tc) + [
            pl.BlockSpec(memory_space=pl.ANY),
            pl.BlockSpec((tc, dh), lambda i: (i, 0)),
            pl.BlockSpec((tc, 2), lambda i: (i, 0)),
            pl.BlockSpec((1, d), lambda i: (0, 0)),
            pl.BlockSpec((1, d), lambda i: (0, 0))],
        out_specs=pl.BlockSpec((tc, d), lambda i: (i, 0)),
        out_shape=jax.ShapeDtypeStruct((n, d), out_dtype),
        scratch_shapes=[pltpu.VMEM((2, 2 * tc, dh), U32), pltpu.SemaphoreType.DMA((2,))],
        compiler_params=_params("arbitrary"),
        name="moe_combine_ln",
    )(pos_tiles, pos_tiles, ys, h_packed, wts.T, g, b)


def _moe_plan(eid, n_exp, tm):
    n = eid.shape[1]
    flat = eid.reshape(-1)
    onehot = (flat[:, None] == jnp.arange(n_exp, dtype=I32)[None, :]).astype(I32)
    csum = jnp.cumsum(onehot, axis=0)
    cnt = csum[-1]
    tiles = (cnt + tm - 1) // tm
    tile_end = jnp.cumsum(tiles)
    tile_start = tile_end - tiles
    grp_start = jnp.cumsum(cnt) - cnt
    pos = jnp.sum(onehot * (csum - 1 + (tile_start * tm)[None, :]), axis=1).reshape(2, n)

    n_tiles = (2 * n) // tm + n_exp
    n_used = tile_end[-1]
    tile_id = jnp.arange(n_tiles, dtype=I32)
    te = jnp.minimum(jnp.searchsorted(tile_end, tile_id, side="right"), n_exp - 1).astype(I32)
    te = jnp.where(tile_id < n_used, te, te[jnp.maximum(n_used - 1, 0)])

    order = jnp.argsort(flat, stable=True).astype(I32)
    base = grp_start[te] + (tile_id - tile_start[te]) * tm
    last = grp_start[te] + cnt[te] - 1
    idx = jnp.minimum(base[:, None] + jnp.arange(tm, dtype=I32)[None, :], last[:, None])
    src = order[jnp.clip(idx, 0, 2 * n - 1).reshape(-1)] % n
    return pos.astype(I32), src.astype(I32), te, n_used.astype(I32).reshape(1)


def _moe_ln(h_packed, eid, wts, w_in, w_out, layer, g, b, *, alpha, out_dtype, tm):
    n_exp = w_in.shape[1]
    pos, src, tile_expert, n_used = _moe_plan(eid, n_exp, tm)
    h1 = _moe_up(h_packed, src, w_in, layer, tile_expert, n_used, tm=tm)
    ys = _moe_down(h1, w_out, layer, tile_expert, n_used, tm=tm)
    return _combine_ln(ys, pos, h_packed, wts, g, b, alpha=alpha, out_dtype=out_dtype, tc=256)


def _pad_heads(a, nh, hd, hdp, axis):
    shape = a.shape
    a = a.reshape(shape[:axis] + (nh, hd) + shape[axis + 1:])
    pad = [(0, 0)] * a.ndim
    pad[axis + 1] = (0, hdp - hd)
    a = jnp.pad(a, pad)
    return a.reshape(shape[:axis] + (nh * hdp,) + shape[axis + 1:])


def kernel(x, rg_w_in, rg_conv_w, rg_conv_b, rg_w_ra, rg_b_ra, rg_w_ix, rg_b_ix, rg_lambda, rg_w_out,
           hg_w_in, hg_norm_w, hg_w_out, hg_lb_logits, router_w, router_b, moe_w_in, moe_w_out, ln_g, ln_b):
    batch, seq, d = x.shape
    n = batch * seq
    depth = ln_g.shape[0]
    alpha = (2.0 * depth) ** 0.25
    nh, hd = rg_w_ra.shape[1], rg_w_ra.shape[2]
    hdp = _round_up(hd, LANE)
    hv = hg_w_out.shape[1]
    hk = (hg_w_in.shape[2] - 2 * hv) // 2
    heads = hv // hg_norm_w.shape[1]
    assert hk == hv, "HGRN2 key and value widths must match"
    moe_tm = min(512, n)
    moe_w_out_bf = moe_w_out.astype(BF16)

    h = x.reshape(n, d)
    h_bf = h.astype(BF16)
    for layer in range(depth):
        j = layer // 2
        if layer % 2 == 0:
            w_in = _pad_heads(rg_w_in[j].astype(BF16).reshape(d, 2, nh * hd), nh, hd, hdp, 2)
            proj = _matmul(h_bf, w_in.reshape(d, 2 * nh * hdp), acts=(jax.nn.gelu, _identity),
                           col_starts=(0, nh * hdp), width=nh * hdp, out_dtype=BF16,
                           tm=1024, tn=1024, name="rg_in_proj")
            pad_vec = lambda v: _pad_heads(v.reshape(1, nh * hd).astype(F32), nh, hd, hdp, 1)
            pad_mat = lambda w: jnp.pad(w, ((0, 0), (0, hdp - hd), (0, hdp - hd))).astype(BF16)
            y = _rg_core(proj, _pad_heads(rg_conv_w[j].astype(F32), nh, hd, hdp, 1), pad_vec(rg_conv_b[j]),
                         pad_mat(rg_w_ra[j]), pad_vec(rg_b_ra[j]), pad_mat(rg_w_ix[j]), pad_vec(rg_b_ix[j]),
                         pad_vec(rg_lambda[j]), batch=batch, seq=seq, tt=512)
            w_out = _pad_heads(rg_w_out[j].astype(BF16), nh, hd, hdp, 0)
            mix = _matmul(y, w_out, acts=(_identity,), col_starts=(0,), width=d, out_dtype=F32,
                          tm=512, tn=1024, name="rg_out_proj")
        else:
            w = hg_w_in[j].astype(BF16)
            qvg = _matmul(h_bf, w, acts=(_silu, _identity, _silu), col_starts=(0, 2 * hk, 2 * hk + hv),
                          width=hk, out_dtype=BF16, tm=1024, tn=1024, name="hg_in_proj")
            log_f, key = _hg_gate_proj(h_bf, w, hg_lb_logits.astype(F32), col_start=hk, width=hk,
                                       layer=layer, tm=1024, tn=1024)
            o = _hg_core(qvg, log_f, key, hg_norm_w[j].reshape(1, -1).astype(F32),
                         batch=batch, seq=seq, heads=heads, tt=1024)
            mix = _matmul(o, hg_w_out[j].astype(BF16), acts=(_identity,), col_starts=(0,), width=d,
                          out_dtype=F32, tm=1024, tn=1024, name="hg_out_proj")
        g1, b1 = ln_g[layer, 0].reshape(1, d).astype(F32), ln_b[layer, 0].reshape(1, d).astype(F32)
        g2, b2 = ln_g[layer, 1].reshape(1, d).astype(F32), ln_b[layer, 1].reshape(1, d).astype(F32)
        h_packed, eid, wts = _ln_route(mix, h, g1, b1, router_w, router_b, alpha=alpha, tr=256)
        last = layer == depth - 1
        h = _moe_ln(h_packed, eid, wts, moe_w_in, moe_w_out_bf, layer, g2, b2, alpha=alpha,
                    out_dtype=x.dtype if last else BF16, tm=moe_tm)
        h_bf = h
    return h.reshape(batch, seq, d)
```

```python
import functools

import numpy as np
import jax
import jax.numpy as jnp
from jax import lax
from jax.experimental import pallas as pl
from jax.experimental.pallas import tpu as pltpu

F32 = jnp.float32
BF16 = jnp.bfloat16
I32 = jnp.int32
U32 = jnp.uint32

LANE = 128
SUBLANE = 8
MXU_WIDTH = 256
VMEM_LIMIT_BYTES = 56 * 2**20

RG_C = 8.0
N_GROUPS = 4
LN_EPS = 1e-5
RMS_EPS = 1e-6
HG_CHUNK = 128
HG_BAND = 4
HG_HEADS_PER_STEP = 2
LOG2_E = 1.4426950408889634


def _params(*sem):
    return pltpu.CompilerParams(dimension_semantics=sem, vmem_limit_bytes=VMEM_LIMIT_BYTES)


def _round_up(x, m):
    return (x + m - 1) // m * m


def _silu(x):
    return x * jax.nn.sigmoid(x)


def _identity(x):
    return x


def _softplus(x):
    return jnp.maximum(x, 0.0) + jnp.log1p(jnp.exp(-jnp.abs(x)))


def _dot(a, b):
    return jnp.dot(a, b, preferred_element_type=F32)


def _dot_nt(a, b):
    return lax.dot_general(a, b, (((1,), (1,)), ((), ())), preferred_element_type=F32)


def _dot_tn(a, b):
    return lax.dot_general(a, b, (((0,), (0,)), ((), ())), preferred_element_type=F32)


def _split_bf16(x):
    hi = x.astype(BF16)
    lo = (x - hi.astype(F32)).astype(BF16)
    return hi, lo


def _pack_pairs(y):
    half = y.shape[1] // 2
    bits = lax.bitcast_convert_type(y.astype(BF16).astype(F32), U32)
    return bits[:, :half] | (bits[:, half:] >> 16)


def _unpack_pairs(p):
    hi = lax.bitcast_convert_type(p & jnp.uint32(0xFFFF0000), F32)
    lo = lax.bitcast_convert_type(p << 16, F32)
    return jnp.concatenate([hi, lo], axis=1)


def _mm_body(x_ref, w_ref, o_ref, *, acts, tiles_per_region):
    acc = _dot(x_ref[...], w_ref[...])
    if len(acts) == 1:
        o_ref[...] = acts[0](acc).astype(o_ref.dtype)
        return
    region = pl.program_id(1) // tiles_per_region
    for r, act in enumerate(acts):
        @pl.when(region == r)
        def _(act=act):
            o_ref[...] = act(acc).astype(o_ref.dtype)


def _matmul(x, w, *, acts, col_starts, width, out_dtype, tm, tn, name):
    m, k = x.shape
    tm, tn = min(tm, m), min(tn, width)
    assert m % tm == 0 and width % tn == 0 and all(c % tn == 0 for c in col_starts)
    per = width // tn
    starts = [c // tn for c in col_starts]

    def w_index(i, j):
        col = starts[0] + j
        for r in range(1, len(starts)):
            col = jnp.where(j >= r * per, starts[r] + (j - r * per), col)
        return (0, col)

    return pl.pallas_call(
        functools.partial(_mm_body, acts=acts, tiles_per_region=per),
        grid=(m // tm, per * len(acts)),
        in_specs=[pl.BlockSpec((tm, k), lambda i, j: (i, 0)),
                  pl.BlockSpec((k, tn), w_index)],
        out_specs=pl.BlockSpec((tm, tn), lambda i, j: (i, j)),
        out_shape=jax.ShapeDtypeStruct((m, width * len(acts)), out_dtype),
        compiler_params=_params("arbitrary", "arbitrary"),
        name=name,
    )(x, w)


def _hg_gate_body(x_ref, w_ref, lbl_ref, lf_ref, k_ref, *, layer):
    logits = lbl_ref[...]
    mx = jnp.max(logits, axis=0, keepdims=True)
    ex = jnp.exp(logits - mx)
    probs = ex / jnp.sum(ex, axis=0, keepdims=True)
    lb = jnp.zeros_like(mx)
    for l in range(1, layer + 1):
        lb = lb + probs[l:l + 1, :]
    z = _dot(x_ref[...], w_ref[...])
    t = jnp.exp(-jnp.abs(z))
    log_sig = jnp.minimum(z, 0.0) - jnp.log(1.0 + t)
    la = jnp.log(lb)
    lc = jnp.log1p(-lb) + log_sig
    lf_ref[...] = LOG2_E * (jnp.maximum(la, lc) + jnp.log(1.0 + jnp.exp(-jnp.abs(la - lc))))
    sig_neg = jnp.where(z >= 0.0, t, 1.0) / (1.0 + t)
    k_ref[...] = ((1.0 - lb) * sig_neg).astype(k_ref.dtype)


def _hg_gate_proj(x, w, lb_logits, *, col_start, width, layer, tm, tn):
    m, k = x.shape
    depth = lb_logits.shape[0]
    tm, tn = min(tm, m), min(tn, width)
    assert m % tm == 0 and width % tn == 0 and col_start % tn == 0
    off = col_start // tn
    return pl.pallas_call(
        functools.partial(_hg_gate_body, layer=layer),
        grid=(m // tm, width // tn),
        in_specs=[pl.BlockSpec((tm, k), lambda i, j: (i, 0)),
                  pl.BlockSpec((k, tn), lambda i, j: (0, off + j)),
                  pl.BlockSpec((depth, tn), lambda i, j: (0, j))],
        out_specs=[pl.BlockSpec((tm, tn), lambda i, j: (i, j)),
                   pl.BlockSpec((tm, tn), lambda i, j: (i, j))],
        out_shape=[jax.ShapeDtypeStruct((m, width), F32), jax.ShapeDtypeStruct((m, width), BF16)],
        compiler_params=_params("arbitrary", "arbitrary"),
        name="hg_gate_proj",
    )(x, w, lb_logits)


def _rg_body(gate_ref, xr_ref, cw_ref, cb_ref, wra_ref, bra_ref, wix_ref, bix_ref, lam_ref,
             y_ref, xpad, a_scr, u_scr, hcar, *, tt, conv_w):
    @pl.when(pl.program_id(2) == 0)
    def _():
        xpad[0:SUBLANE, :] = jnp.zeros((SUBLANE, xpad.shape[1]), F32)
        hcar[...] = jnp.zeros_like(hcar)

    x = xr_ref[...].astype(F32)
    xpad[SUBLANE:SUBLANE + tt, :] = x
    xc = cb_ref[...] + cw_ref[conv_w - 1:conv_w, :] * x
    for j in range(conv_w - 1):
        xc = xc + cw_ref[j:j + 1, :] * xpad[pl.ds(SUBLANE - (conv_w - 1) + j, tt), :]
    xpad[0:SUBLANE, :] = x[tt - SUBLANE:tt, :]

    xcb = xc.astype(BF16)
    r = jax.nn.sigmoid(_dot(xcb, wra_ref[...]) + bra_ref[...])
    ig = jax.nn.sigmoid(_dot(xcb, wix_ref[...]) + bix_ref[...])
    log_a = (-RG_C * _softplus(-lam_ref[...])) * r
    a = jnp.exp(log_a)
    a_scr[...] = a
    u_scr[...] = jnp.sqrt(-jnp.tanh(log_a) * (a * a + 1.0)) * (ig * xc)

    row = lax.broadcasted_iota(I32, (SUBLANE, xpad.shape[1]), 0)

    def block(i, hprev):
        r0 = pl.multiple_of(i * SUBLANE, SUBLANE)
        a = a_scr[pl.ds(r0, SUBLANE), :]
        u = u_scr[pl.ds(r0, SUBLANE), :]
        for s in (1, 2, 4):
            keep = row >= s
            u = jnp.where(keep, a * pltpu.roll(u, s, axis=0) + u, u)
            a = jnp.where(keep, a * pltpu.roll(a, s, axis=0), a)
        h = a * hprev + u
        u_scr[pl.ds(r0, SUBLANE), :] = h
        return jnp.broadcast_to(h[SUBLANE - 1:SUBLANE, :], h.shape)

    hcar[...] = lax.fori_loop(0, tt // SUBLANE, block, hcar[...])
    y_ref[...] = (u_scr[...] * gate_ref[...].astype(F32)).astype(y_ref.dtype)


def _rg_core(proj, conv_w, conv_b, w_ra, b_ra, w_ix, b_ix, lam, *, batch, seq, tt):
    nh, hd = w_ra.shape[0], w_ra.shape[1]
    cw = conv_w.shape[0]
    tt = min(tt, seq)
    assert seq % tt == 0 and tt % SUBLANE == 0 and cw - 1 <= SUBLANE
    proj3 = proj.reshape(batch, seq, 2 * nh * hd)
    vec = lambda: pl.BlockSpec((1, hd), lambda b, h, t: (0, h))
    y = pl.pallas_call(
        functools.partial(_rg_body, tt=tt, conv_w=cw),
        grid=(batch, nh, seq // tt),
        in_specs=[pl.BlockSpec((None, tt, hd), lambda b, h, t: (b, t, h)),
                  pl.BlockSpec((None, tt, hd), lambda b, h, t: (b, t, nh + h)),
                  pl.BlockSpec((cw, hd), lambda b, h, t: (0, h)),
                  vec(),
                  pl.BlockSpec((None, hd, hd), lambda b, h, t: (h, 0, 0)),
                  vec(),
                  pl.BlockSpec((None, hd, hd), lambda b, h, t: (h, 0, 0)),
                  vec(),
                  vec()],
        out_specs=pl.BlockSpec((None, tt, hd), lambda b, h, t: (b, t, h)),
        out_shape=jax.ShapeDtypeStruct((batch, seq, nh * hd), BF16),
        scratch_shapes=[pltpu.VMEM((tt + SUBLANE, hd), F32),
                        pltpu.VMEM((tt, hd), F32),
                        pltpu.VMEM((tt, hd), F32),
                        pltpu.VMEM((SUBLANE, hd), F32)],
        compiler_params=_params("arbitrary", "arbitrary", "arbitrary"),
        name="rg_core",
    )(proj3, proj3, conv_w, conv_b, w_ra, b_ra, w_ix, b_ix, lam)
    return y.reshape(batch * seq, nh * hd)


def _hg_tables(chunk, band, width):
    t = np.arange(chunk)[:, None]
    s = np.arange(chunk)[None, :]
    masks, signs = [], []
    half = chunk // 2
    while half >= band:
        same_parent = (t // (2 * half)) == (s // (2 * half))
        split = ((t // half) % 2 == 1) & ((s // half) % 2 == 0)
        masks.append((same_parent & split & (t - s >= band)).astype(np.float32))
        signs.append(np.broadcast_to(np.where((t // half) % 2 == 1, 1.0, -1.0), (chunk, width)).astype(np.float32))
        half //= 2
    return np.stack(masks), np.stack(signs)


def _hg_body(q_ref, v_ref, g_ref, lf_ref, k_ref, nw_ref, tri_ref, mask_ref, sign_ref, o_ref,
             st_ref, kpad, bpad, vpad, *, tt, chunk, band, heads):
    kd = LANE

    @pl.when(pl.program_id(2) == 0)
    def _():
        st_ref[...] = jnp.zeros_like(st_ref)

    zpad = jnp.zeros((SUBLANE, heads * kd), F32)
    kpad[0:SUBLANE, :] = zpad
    bpad[0:SUBLANE, :] = zpad
    vpad[0:SUBLANE, :] = zpad
    tri = tri_ref[...]
    row = lax.broadcasted_iota(I32, (chunk, 1), 0)

    def one_chunk(c, carry):
        r0 = pl.multiple_of(c * chunk, chunk)
        q2 = q_ref[pl.ds(r0, chunk), :].astype(F32)
        k2 = k_ref[pl.ds(r0, chunk), :].astype(F32)
        vb2 = v_ref[pl.ds(r0, chunk), :]
        g2 = g_ref[pl.ds(r0, chunk), :].astype(F32)
        lf_hi, lf_lo = _split_bf16(lf_ref[pl.ds(r0, chunk), :])
        b2 = _dot(tri, lf_hi) + _dot(tri, lf_lo)
        kpad[SUBLANE:SUBLANE + chunk, :] = k2
        bpad[SUBLANE:SUBLANE + chunk, :] = b2
        vpad[SUBLANE:SUBLANE + chunk, :] = vb2.astype(F32)

        outs = []
        for h in range(heads):
            sl = slice(h * kd, (h + 1) * kd)
            q, k, vb, b = q2[:, sl], k2[:, sl], vb2[:, sl], b2[:, sl]
            b_last = b[chunk - 1:chunk, :]

            st = st_ref[h]
            o = _dot_nt((q * jnp.exp2(b)).astype(BF16), st.astype(BF16))
            k_dec = (k * jnp.exp2(b_last - b)).astype(BF16)
            st_ref[h] = st * jnp.exp2(b_last) + _dot_tn(vb, k_dec)

            scores = jnp.zeros((chunk, chunk), BF16)
            half, level = chunk // 2, 0
            while half >= band:
                if half >= SUBLANE:
                    rows = []
                    for p0 in range(0, chunk, 2 * half):
                        r = b[p0 + half - 1:p0 + half, :]
                        lo_, mid_, hi_ = p0, p0 + half, p0 + 2 * half
                        rows.append(k[lo_:mid_] * jnp.exp2(r - b[lo_:mid_]))
                        rows.append(q[mid_:hi_] * jnp.exp2(b[mid_:hi_] - r))
                    both = jnp.concatenate(rows, axis=0).astype(BF16)
                else:
                    b3 = b.reshape(chunk // (2 * half), 2 * half, kd)
                    ref = jnp.broadcast_to(b3[:, half - 1:half, :], b3.shape).reshape(chunk, kd)
                    sign = sign_ref[level]
                    e = jnp.exp2((b - ref) * sign)
                    both = (jnp.where(sign > 0.0, q, k) * e).astype(BF16)
                scores = scores + mask_ref[level] * _dot_nt(both, both).astype(BF16)
                half //= 2
                level += 1
            o = o + _dot(scores, vb)

            o = o + jnp.sum(q * k, axis=-1, keepdims=True) * vb.astype(F32)
            for d in range(1, band):
                kd_ = kpad[pl.ds(SUBLANE - d, chunk), sl]
                bd_ = bpad[pl.ds(SUBLANE - d, chunk), sl]
                vd_ = vpad[pl.ds(SUBLANE - d, chunk), sl]
                w = q * kd_ * jnp.exp2(jnp.minimum(b - bd_, 0.0))
                sc = jnp.sum(w, axis=-1, keepdims=True)
                o = o + jnp.where(row >= d, sc, 0.0) * vd_

            ms = jnp.mean(o * o, axis=-1, keepdims=True)
            outs.append(o * lax.rsqrt(ms + RMS_EPS) * nw_ref[...] * g2[:, sl])
        y = outs[0] if heads == 1 else jnp.concatenate(outs, axis=1)
        o_ref[pl.ds(r0, chunk), :] = y.astype(o_ref.dtype)
        return carry

    lax.fori_loop(0, tt // chunk, one_chunk, 0, unroll=True)


def _hg_core(qvg, log_f, key, norm_w, *, batch, seq, heads, tt):
    kd = norm_w.shape[-1]
    assert kd == LANE, "head key/value width must equal the lane width"
    hps = HG_HEADS_PER_STEP if heads % HG_HEADS_PER_STEP == 0 else 1
    tt = min(tt, seq)
    chunk = min(HG_CHUNK, tt)
    assert seq % tt == 0 and tt % chunk == 0 and chunk % (2 * SUBLANE) == 0 and chunk & (chunk - 1) == 0
    masks_np, signs_np = _hg_tables(chunk, HG_BAND, kd)
    masks = jnp.asarray(masks_np, dtype=BF16)
    signs = jnp.asarray(signs_np)
    tri = jnp.asarray(np.tril(np.ones((chunk, chunk), np.float32)), dtype=BF16)
    wd = hps * kd
    groups = heads // hps
    qvg3 = qvg.reshape(batch, seq, 3 * heads * kd)
    lf3 = log_f.reshape(batch, seq, heads * kd)
    k3 = key.reshape(batch, seq, heads * kd)
    blk = lambda off: pl.BlockSpec((None, tt, wd), lambda b, h, t, off=off: (b, t, off + h))
    const = lambda a: pl.BlockSpec(a.shape, lambda b, h, t: (0,) * a.ndim)
    out = pl.pallas_call(
        functools.partial(_hg_body, tt=tt, chunk=chunk, band=HG_BAND, heads=hps),
        grid=(batch, groups, seq // tt),
        in_specs=[blk(0), blk(groups), blk(2 * groups), blk(0), blk(0),
                  const(norm_w), const(tri), const(masks), const(signs)],
        out_specs=blk(0),
        out_shape=jax.ShapeDtypeStruct((batch, seq, heads * kd), BF16),
        scratch_shapes=[pltpu.VMEM((hps, kd, kd), F32),
                        pltpu.VMEM((chunk + SUBLANE, wd), F32),
                        pltpu.VMEM((chunk + SUBLANE, wd), F32),
                        pltpu.VMEM((chunk + SUBLANE, wd), F32)],
        compiler_params=_params("arbitrary", "arbitrary", "arbitrary"),
        name="hg_core",
    )(qvg3, qvg3, qvg3, lf3, k3, norm_w, tri, masks, signs)
    return out.reshape(batch * seq, heads * kd)


def _layer_norm(x, g, b):
    mu = jnp.mean(x, axis=-1, keepdims=True)
    xc = x - mu
    var = jnp.mean(xc * xc, axis=-1, keepdims=True)
    return xc * lax.rsqrt(var + LN_EPS) * g + b


def _first_max(vals):
    best, idx = vals[0], jnp.zeros(vals[0].shape, I32)
    for i in range(1, len(vals)):
        better = vals[i] > best
        best = jnp.where(better, vals[i], best)
        idx = jnp.where(better, i, idx)
    return best, idx


def _top2(vals):
    top1, i1 = _first_max(vals)
    rest = [jnp.where(i1 == i, -jnp.inf, vals[i]) for i in range(len(vals))]
    top2, i2 = _first_max(rest)
    return top1, i1, top2, i2


def _route(logit_rows, n_groups):
    n_exp = len(logit_rows)
    per = n_exp // n_groups
    mx = functools.reduce(jnp.maximum, logit_rows)
    ex = [jnp.exp(l - mx) for l in logit_rows]
    den = functools.reduce(lambda a, b: a + b, ex)
    p = [e / den for e in ex]
    scores = []
    for g in range(n_groups):
        top1, _, top2, _ = _top2(p[g * per:(g + 1) * per])
        scores.append(top1 + top2)
    _, g_sel = _first_max(scores)
    in_group = []
    for i in range(per):
        val = p[i]
        for g in range(1, n_groups):
            val = jnp.where(g_sel == g, p[g * per + i], val)
        in_group.append(val)
    p1, i1, p2, i2 = _top2(in_group)
    tot = p1 + p2
    return g_sel * per + i1, g_sel * per + i2, p1 / tot, p2 / tot


def _ln_route_body(mix_ref, h_ref, g_ref, b_ref, rwh_ref, rwl_ref, rb_ref, o_ref, eid_ref, wt_ref,
                   *, alpha, n_exp, n_groups):
    y = _layer_norm(alpha * h_ref[...].astype(F32) + mix_ref[...], g_ref[...], b_ref[...])
    o_ref[...] = _pack_pairs(y)
    y_hi, y_lo = _split_bf16(y)
    logits = _dot(y_hi, rwh_ref[...]) + (_dot(y_hi, rwl_ref[...]) + _dot(y_lo, rwh_ref[...]))
    lt = logits.T + rb_ref[...]
    e1, e2, w1, w2 = _route([lt[e:e + 1, :] for e in range(n_exp)], n_groups)
    eid_ref[0:1, :] = e1
    eid_ref[1:2, :] = e2
    wt_ref[0:1, :] = w1
    wt_ref[1:2, :] = w2


def _ln_route(mix, h, g, b, router_w, router_b, *, alpha, tr):
    n, d = mix.shape
    n_exp = router_w.shape[1]
    tr = min(tr, n)
    assert n % tr == 0 and n_exp <= LANE and n_exp % N_GROUPS == 0 and d % (2 * LANE) == 0
    rw = jnp.zeros((d, LANE), F32).at[:, :n_exp].set(router_w.astype(F32))
    rw_hi = rw.astype(BF16)
    rw_lo = (rw - rw_hi.astype(F32)).astype(BF16)
    rb = jnp.zeros((LANE, 1), F32).at[:n_exp, 0].set(router_b.astype(F32))
    row = lambda: pl.BlockSpec((tr, d), lambda i: (i, 0))
    full = lambda shape: pl.BlockSpec(shape, lambda i: (0, 0))
    return pl.pallas_call(
        functools.partial(_ln_route_body, alpha=alpha, n_exp=n_exp, n_groups=N_GROUPS),
        grid=(n // tr,),
        in_specs=[row(), row(), full((1, d)), full((1, d)), full((d, LANE)), full((d, LANE)),
                  full((LANE, 1))],
        out_specs=[pl.BlockSpec((tr, d // 2), lambda i: (i, 0)),
                   pl.BlockSpec((2, tr), lambda i: (0, i)), pl.BlockSpec((2, tr), lambda i: (0, i))],
        out_shape=[jax.ShapeDtypeStruct((n, d // 2), U32), jax.ShapeDtypeStruct((2, n), I32),
                   jax.ShapeDtypeStruct((2, n), F32)],
        compiler_params=_params("arbitrary"),
        name="ln_route",
    )(mix, h, g, b, rw_hi, rw_lo, rb)


def _start_row_gather(src_hbm, idx_ref, dst, sem, first_row, n_rows):
    for r in range(first_row, first_row + n_rows):
        pltpu.make_async_copy(src_hbm.at[pl.ds(idx_ref[0, 0, r], 1)], dst.at[pl.ds(r, 1)], sem).start()


def _wait_row_gather(src_hbm, dst, sem, n_rows):
    for r in range(n_rows):
        pltpu.make_async_copy(src_hbm.at[pl.ds(0, 1)], dst.at[pl.ds(r, 1)], sem).wait()


def _pipelined_gather(src_hbm, idx_ref, idx_next_ref, buf, sem, n_rows):
    i = pl.program_id(0)
    slot = lax.rem(i, 2)

    @pl.when(i == 0)
    def _():
        _start_row_gather(src_hbm, idx_ref, buf.at[0], sem.at[0], 0, n_rows)

    @pl.when(i + 1 < pl.num_programs(0))
    def _():
        _start_row_gather(src_hbm, idx_next_ref, buf.at[1 - slot], sem.at[1 - slot], 0, n_rows)

    _wait_row_gather(src_hbm, buf.at[slot], sem.at[slot], n_rows)
    return slot


def _idx_specs(tiles, width):
    return [pl.BlockSpec((1, 1, width), lambda i, *_: (i, 0, 0), memory_space=pltpu.SMEM),
            pl.BlockSpec((1, 1, width), lambda i, *_: (jnp.minimum(i + 1, tiles - 1), 0, 0),
                         memory_space=pltpu.SMEM)]


def _load_expert_weights(w_hbm, layer, expert, w_bf, stage, wsem, *, cols):
    ff = w_bf.shape[2]
    n_chunks = (2 * ff) // cols

    def chunk_copy(c):
        return pltpu.make_async_copy(w_hbm.at[layer, expert, :, pl.ds(c * cols, cols)],
                                     stage.at[c % 2], wsem.at[c % 2])

    chunk_copy(0).start()
    for c in range(n_chunks):
        if c + 1 < n_chunks:
            chunk_copy(c + 1).start()
        chunk_copy(c).wait()
        half, col = divmod(c * cols, ff)
        w_bf[half, :, col:col + cols] = stage[c % 2].astype(BF16)


def _moe_up_body(te_ref, nu_ref, src_ref, src_next_ref, h_hbm, w_hbm, o_ref, buf, sem, w_bf, stage, wsem, x_bf,
                 *, tm, layer, cols, col_blocks):
    i = pl.program_id(0)
    n_used = nu_ref[0]
    last = pl.num_programs(0) - 1
    slot = lax.rem(i, 2)
    expert = te_ref[i]
    ff = o_ref.shape[1]
    cb = ff // col_blocks
    row_split = [(c * tm) // col_blocks for c in range(col_blocks + 1)]

    @pl.when(i == 0)
    def _():
        _start_row_gather(h_hbm, src_ref, buf.at[0], sem.at[0], 0, tm)

    @pl.when((i < n_used) & ((i == 0) | (expert != te_ref[jnp.maximum(i - 1, 0)])))
    def _():
        _load_expert_weights(w_hbm, layer, expert, w_bf, stage, wsem, cols=cols)

    @pl.when(i < n_used)
    def _():
        _wait_row_gather(h_hbm, buf.at[slot], sem.at[slot], tm)
        x_bf[...] = _unpack_pairs(buf[slot]).astype(BF16)
        for c in range(col_blocks):
            _start_row_gather(h_hbm, src_next_ref, buf.at[1 - slot], sem.at[1 - slot], row_split[c],
                              row_split[c + 1] - row_split[c])
            x = x_bf[...]
            gate = _dot(x, w_bf[0, :, c * cb:(c + 1) * cb])
            up = _dot(x, w_bf[1, :, c * cb:(c + 1) * cb])
            o_ref[:, c * cb:(c + 1) * cb] = (_silu(gate) * up).astype(o_ref.dtype)

    @pl.when(i >= n_used)
    def _():
        o_ref[...] = jnp.zeros_like(o_ref)

    @pl.when(i == n_used)
    def _():
        _wait_row_gather(h_hbm, buf.at[slot], sem.at[slot], tm)

    @pl.when((i == last) & (n_used > last))
    def _():
        _wait_row_gather(h_hbm, buf.at[1 - slot], sem.at[1 - slot], tm)


def _moe_up(h_packed, src_token, w_in, layer, tile_expert, n_used, *, tm):
    n, dh = h_packed.shape
    d = 2 * dh
    p = src_token.shape[0]
    ff = w_in.shape[3] // 2
    cols = min(LANE, ff)
    col_blocks = ff // MXU_WIDTH if ff % MXU_WIDTH == 0 else 1
    assert p % tm == 0 and ff % cols == 0
    tiles = p // tm
    src_tiles = src_token.reshape(tiles, 1, tm)
    return pl.pallas_call(
        functools.partial(_moe_up_body, tm=tm, layer=layer, cols=cols, col_blocks=col_blocks),
        grid_spec=pltpu.PrefetchScalarGridSpec(
            num_scalar_prefetch=2,
            grid=(tiles,),
            in_specs=_idx_specs(tiles, tm) + [pl.BlockSpec(memory_space=pl.ANY),
                                              pl.BlockSpec(memory_space=pl.ANY)],
            out_specs=pl.BlockSpec((tm, ff), lambda i, te, nu: (i, 0)),
            scratch_shapes=[pltpu.VMEM((2, tm, dh), U32), pltpu.SemaphoreType.DMA((2,)),
                            pltpu.VMEM((2, d, ff), BF16), pltpu.VMEM((2, d, cols), w_in.dtype),
                            pltpu.SemaphoreType.DMA((2,)), pltpu.VMEM((tm, d), BF16)]),
        out_shape=jax.ShapeDtypeStruct((p, ff), BF16),
        compiler_params=_params("arbitrary"),
        name="moe_up",
    )(tile_expert, n_used, src_tiles, src_tiles, h_packed, w_in)


def _moe_down_body(te_ref, nu_ref, x_ref, w_ref, o_ref):
    @pl.when(pl.program_id(0) < nu_ref[0])
    def _():
        o_ref[...] = _pack_pairs(_dot(x_ref[...], w_ref[...]))

    @pl.when(pl.program_id(0) >= nu_ref[0])
    def _():
        o_ref[...] = jnp.zeros_like(o_ref)


def _moe_down(h1, w_out, layer, tile_expert, n_used, *, tm):
    p, ff = h1.shape
    d = w_out.shape[3]
    assert p % tm == 0
    return pl.pallas_call(
        _moe_down_body,
        grid_spec=pltpu.PrefetchScalarGridSpec(
            num_scalar_prefetch=2,
            grid=(p // tm,),
            in_specs=[pl.BlockSpec((tm, ff), lambda i, te, nu: (i, 0)),
                      pl.BlockSpec((None, None, ff, d), lambda i, te, nu: (layer, te[i], 0, 0))],
            out_specs=pl.BlockSpec((tm, d // 2), lambda i, te, nu: (i, 0))),
        out_shape=jax.ShapeDtypeStruct((p, d // 2), U32),
        compiler_params=_params("arbitrary"),
        name="moe_down",
    )(tile_expert, n_used, h1, w_out)


def _combine_ln_body(pos_ref, pos_next_ref, ys_hbm, h_ref, wt_ref, g_ref, b_ref, o_ref, buf, sem,
                     *, tc, alpha):
    slot = _pipelined_gather(ys_hbm, pos_ref, pos_next_ref, buf, sem, 2 * tc)
    wt = wt_ref[...]
    rows = buf[slot]
    ffn = wt[:, 0:1] * _unpack_pairs(rows[0:tc]) + wt[:, 1:2] * _unpack_pairs(rows[tc:2 * tc])
    x = alpha * _unpack_pairs(h_ref[...]) + ffn
    o_ref[...] = _layer_norm(x, g_ref[...], b_ref[...]).astype(o_ref.dtype)


def _combine_ln(ys, pos, h_packed, wts, g, b, *, alpha, out_dtype, tc):
    n, dh = h_packed.shape
    d = 2 * dh
    tc = min(tc, n)
    assert n % tc == 0
    pos_tiles = jnp.concatenate([pos[0].reshape(n // tc, 1, tc), pos[1].reshape(n // tc, 1, tc)], axis=2)
    return pl.pallas_call(
        functools.partial(_combine_ln_body, tc=tc, alpha=alpha),
        grid=(n // tc,),
        in_specs=_idx_specs(n // tc, 2 * tc) + [
            pl.BlockSpec(memory_space=pl.ANY),
            pl.BlockSpec((tc, dh), lambda i: (i, 0)),
            pl.BlockSpec((tc, 2), lambda i: (i, 0)),
            pl.BlockSpec((1, d), lambda i: (0, 0)),
            pl.BlockSpec((1, d), lambda i: (0, 0))],
        out_specs=pl.BlockSpec((tc, d), lambda i: (i, 0)),
        out_shape=jax.ShapeDtypeStruct((n, d), out_dtype),
        scratch_shapes=[pltpu.VMEM((2, 2 * tc, dh), U32), pltpu.SemaphoreType.DMA((2,))],
        compiler_params=_params("arbitrary"),
        name="moe_combine_ln",
    )(pos_tiles, pos_tiles, ys, h_packed, wts.T, g, b)


def _moe_plan(eid, n_exp, tm):
    n = eid.shape[1]
    flat = eid.reshape(-1)
    onehot = (flat[:, None] == jnp.arange(n_exp, dtype=I32)[None, :]).astype(I32)
    csum = jnp.cumsum(onehot, axis=0)
    cnt = csum[-1]
    tiles = (cnt + tm - 1) // tm
    tile_end = jnp.cumsum(tiles)
    tile_start = tile_end - tiles
    grp_start = jnp.cumsum(cnt) - cnt
    pos = jnp.sum(onehot * (csum - 1 + (tile_start * tm)[None, :]), axis=1).reshape(2, n)

    n_tiles = (2 * n) // tm + n_exp
    n_used = tile_end[-1]
    tile_id = jnp.arange(n_tiles, dtype=I32)
    te = jnp.minimum(jnp.searchsorted(tile_end, tile_id, side="right"), n_exp - 1).astype(I32)
    te = jnp.where(tile_id < n_used, te, te[jnp.maximum(n_used - 1, 0)])

    order = jnp.argsort(flat, stable=True).astype(I32)
    base = grp_start[te] + (tile_id - tile_start[te]) * tm
    last = grp_start[te] + cnt[te] - 1
    idx = jnp.minimum(base[:, None] + jnp.arange(tm, dtype=I32)[None, :], last[:, None])
    src = order[jnp.clip(idx, 0, 2 * n - 1).reshape(-1)] % n
    return pos.astype(I32), src.astype(I32), te, n_used.astype(I32).reshape(1)


def _moe_ln(h_packed, eid, wts, w_in, w_out, layer, g, b, *, alpha, out_dtype, tm):
    n_exp = w_in.shape[1]
    pos, src, tile_expert, n_used = _moe_plan(eid, n_exp, tm)
    h1 = _moe_up(h_packed, src, w_in, layer, tile_expert, n_used, tm=tm)
    ys = _moe_down(h1, w_out, layer, tile_expert, n_used, tm=tm)
    return _combine_ln(ys, pos, h_packed, wts, g, b, alpha=alpha, out_dtype=out_dtype, tc=256)


def _pad_heads(a, nh, hd, hdp, axis):
    shape = a.shape
    a = a.reshape(shape[:axis] + (nh, hd) + shape[axis + 1:])
    pad = [(0, 0)] * a.ndim
    pad[axis + 1] = (0, hdp - hd)
    a = jnp.pad(a, pad)
    return a.reshape(shape[:axis] + (nh * hdp,) + shape[axis + 1:])


def kernel(x, rg_w_in, rg_conv_w, rg_conv_b, rg_w_ra, rg_b_ra, rg_w_ix, rg_b_ix, rg_lambda, rg_w_out,
           hg_w_in, hg_norm_w, hg_w_out, hg_lb_logits, router_w, router_b, moe_w_in, moe_w_out, ln_g, ln_b):
    batch, seq, d = x.shape
    n = batch * seq
    depth = ln_g.shape[0]
    alpha = (2.0 * depth) ** 0.25
    nh, hd = rg_w_ra.shape[1], rg_w_ra.shape[2]
    hdp = _round_up(hd, LANE)
    hv = hg_w_out.shape[1]
    hk = (hg_w_in.shape[2] - 2 * hv) // 2
    heads = hv // hg_norm_w.shape[1]
    assert hk == hv, "HGRN2 key and value widths must match"
    moe_tm = min(512, n)
    moe_w_out_bf = moe_w_out.astype(BF16)

    h = x.reshape(n, d)
    h_bf = h.astype(BF16)
    for layer in range(depth):
        j = layer // 2
        if layer % 2 == 0:
            w_in = _pad_heads(rg_w_in[j].astype(BF16).reshape(d, 2, nh * hd), nh, hd, hdp, 2)
            proj = _matmul(h_bf, w_in.reshape(d, 2 * nh * hdp), acts=(jax.nn.gelu, _identity),
                           col_starts=(0, nh * hdp), width=nh * hdp, out_dtype=BF16,
                           tm=1024, tn=1024, name="rg_in_proj")
            pad_vec = lambda v: _pad_heads(v.reshape(1, nh * hd).astype(F32), nh, hd, hdp, 1)
            pad_mat = lambda w: jnp.pad(w, ((0, 0), (0, hdp - hd), (0, hdp - hd))).astype(BF16)
            y = _rg_core(proj, _pad_heads(rg_conv_w[j].astype(F32), nh, hd, hdp, 1), pad_vec(rg_conv_b[j]),
                         pad_mat(rg_w_ra[j]), pad_vec(rg_b_ra[j]), pad_mat(rg_w_ix[j]), pad_vec(rg_b_ix[j]),
                         pad_vec(rg_lambda[j]), batch=batch, seq=seq, tt=512)
            w_out = _pad_heads(rg_w_out[j].astype(BF16), nh, hd, hdp, 0)
            mix = _matmul(y, w_out, acts=(_identity,), col_starts=(0,), width=d, out_dtype=F32,
                          tm=512, tn=1024, name="rg_out_proj")
        else:
            w = hg_w_in[j].astype(BF16)
            qvg = _matmul(h_bf, w, acts=(_silu, _identity, _silu), col_starts=(0, 2 * hk, 2 * hk + hv),
                          width=hk, out_dtype=BF16, tm=1024, tn=1024, name="hg_in_proj")
            log_f, key = _hg_gate_proj(h_bf, w, hg_lb_logits.astype(F32), col_start=hk, width=hk,
                                       layer=layer, tm=1024, tn=1024)
            o = _hg_core(qvg, log_f, key, hg_norm_w[j].reshape(1, -1).astype(F32),
                         batch=batch, seq=seq, heads=heads, tt=1024)
            mix = _matmul(o, hg_w_out[j].astype(BF16), acts=(_identity,), col_starts=(0,), width=d,
                          out_dtype=F32, tm=1024, tn=1024, name="hg_out_proj")
        g1, b1 = ln_g[layer, 0].reshape(1, d).astype(F32), ln_b[layer, 0].reshape(1, d).astype(F32)
        g2, b2 = ln_g[layer, 1].reshape(1, d).astype(F32), ln_b[layer, 1].reshape(1, d).astype(F32)
        h_packed, eid, wts = _ln_route(mix, h, g1, b1, router_w, router_b, alpha=alpha, tr=256)
        last = layer == depth - 1
        h = _moe_ln(h_packed, eid, wts, moe_w_in, moe_w_out_bf, layer, g2, b2, alpha=alpha,
                    out_dtype=x.dtype if last else BF16, tm=moe_tm)
        h_bf = h
    return h.reshape(batch, seq, d)
```

```python
import functools

import numpy as np
import jax
import jax.numpy as jnp
from jax import lax
from jax.experimental import pallas as pl
from jax.experimental.pallas import tpu as pltpu

F32 = jnp.float32
BF16 = jnp.bfloat16
I32 = jnp.int32
U32 = jnp.uint32

LANE = 128
SUBLANE = 8
MXU_WIDTH = 256
VMEM_LIMIT_BYTES = 56 * 2**20

RG_C = 8.0
N_GROUPS = 4
LN_EPS = 1e-5
RMS_EPS = 1e-6
HG_CHUNK = 128
HG_BAND = 4
HG_HEADS_PER_STEP = 2
WEIGHT_STAGE_SLOTS = 4
LOG2_E = 1.4426950408889634


def _params(*sem):
    return pltpu.CompilerParams(dimension_semantics=sem, vmem_limit_bytes=VMEM_LIMIT_BYTES)


def _round_up(x, m):
    return (x + m - 1) // m * m


def _silu(x):
    return x * jax.nn.sigmoid(x)


def _identity(x):
    return x


def _softplus(x):
    return jnp.maximum(x, 0.0) + jnp.log1p(jnp.exp(-jnp.abs(x)))


def _dot(a, b):
    return jnp.dot(a, b, preferred_element_type=F32)


def _dot_nt(a, b):
    return lax.dot_general(a, b, (((1,), (1,)), ((), ())), preferred_element_type=F32)


def _dot_tn(a, b):
    return lax.dot_general(a, b, (((0,), (0,)), ((), ())), preferred_element_type=F32)


def _split_bf16(x):
    hi = x.astype(BF16)
    lo = (x - hi.astype(F32)).astype(BF16)
    return hi, lo


def _pack_pairs(y):
    half = y.shape[1] // 2
    bits = lax.bitcast_convert_type(y.astype(BF16).astype(F32), U32)
    return bits[:, :half] | (bits[:, half:] >> 16)


def _unpack_pairs(p):
    hi = lax.bitcast_convert_type(p & jnp.uint32(0xFFFF0000), F32)
    lo = lax.bitcast_convert_type(p << 16, F32)
    return jnp.concatenate([hi, lo], axis=1)


def _mm_body(x_ref, w_ref, o_ref, *, acts, tiles_per_region):
    acc = _dot(x_ref[...], w_ref[...])
    if len(acts) == 1:
        o_ref[...] = acts[0](acc).astype(o_ref.dtype)
        return
    region = pl.program_id(1) // tiles_per_region
    for r, act in enumerate(acts):
        @pl.when(region == r)
        def _(act=act):
            o_ref[...] = act(acc).astype(o_ref.dtype)


def _matmul(x, w, *, acts, col_starts, width, out_dtype, tm, tn, name):
    m, k = x.shape
    tm, tn = min(tm, m), min(tn, width)
    assert m % tm == 0 and width % tn == 0 and all(c % tn == 0 for c in col_starts)
    per = width // tn
    starts = [c // tn for c in col_starts]

    def w_index(i, j):
        col = starts[0] + j
        for r in range(1, len(starts)):
            col = jnp.where(j >= r * per, starts[r] + (j - r * per), col)
        return (0, col)

    return pl.pallas_call(
        functools.partial(_mm_body, acts=acts, tiles_per_region=per),
        grid=(m // tm, per * len(acts)),
        in_specs=[pl.BlockSpec((tm, k), lambda i, j: (i, 0)),
                  pl.BlockSpec((k, tn), w_index)],
        out_specs=pl.BlockSpec((tm, tn), lambda i, j: (i, j)),
        out_shape=jax.ShapeDtypeStruct((m, width * len(acts)), out_dtype),
        compiler_params=_params("arbitrary", "arbitrary"),
        name=name,
    )(x, w)


def _hg_gate_body(x_ref, w_ref, lbl_ref, lf_ref, k_ref, *, layer):
    logits = lbl_ref[...]
    mx = jnp.max(logits, axis=0, keepdims=True)
    ex = jnp.exp(logits - mx)
    probs = ex / jnp.sum(ex, axis=0, keepdims=True)
    lb = jnp.zeros_like(mx)
    for l in range(1, layer + 1):
        lb = lb + probs[l:l + 1, :]
    z = _dot(x_ref[...], w_ref[...])
    t = jnp.exp(-jnp.abs(z))
    log_sig = jnp.minimum(z, 0.0) - jnp.log(1.0 + t)
    la = jnp.log(lb)
    lc = jnp.log1p(-lb) + log_sig
    lf_ref[...] = LOG2_E * (jnp.maximum(la, lc) + jnp.log(1.0 + jnp.exp(-jnp.abs(la - lc))))
    sig_neg = jnp.where(z >= 0.0, t, 1.0) / (1.0 + t)
    k_ref[...] = ((1.0 - lb) * sig_neg).astype(k_ref.dtype)


def _hg_gate_proj(x, w, lb_logits, *, col_start, width, layer, tm, tn):
    m, k = x.shape
    depth = lb_logits.shape[0]
    tm, tn = min(tm, m), min(tn, width)
    assert m % tm == 0 and width % tn == 0 and col_start % tn == 0
    off = col_start // tn
    return pl.pallas_call(
        functools.partial(_hg_gate_body, layer=layer),
        grid=(m // tm, width // tn),
        in_specs=[pl.BlockSpec((tm, k), lambda i, j: (i, 0)),
                  pl.BlockSpec((k, tn), lambda i, j: (0, off + j)),
                  pl.BlockSpec((depth, tn), lambda i, j: (0, j))],
        out_specs=[pl.BlockSpec((tm, tn), lambda i, j: (i, j)),
                   pl.BlockSpec((tm, tn), lambda i, j: (i, j))],
        out_shape=[jax.ShapeDtypeStruct((m, width), F32), jax.ShapeDtypeStruct((m, width), BF16)],
        compiler_params=_params("arbitrary", "arbitrary"),
        name="hg_gate_proj",
    )(x, w, lb_logits)


def _rg_body(gate_ref, xr_ref, cw_ref, cb_ref, wra_ref, bra_ref, wix_ref, bix_ref, lam_ref,
             y_ref, xpad, a_scr, u_scr, hcar, *, tt, conv_w):
    @pl.when(pl.program_id(2) == 0)
    def _():
        xpad[0:SUBLANE, :] = jnp.zeros((SUBLANE, xpad.shape[1]), F32)
        hcar[...] = jnp.zeros_like(hcar)

    x = xr_ref[...].astype(F32)
    xpad[SUBLANE:SUBLANE + tt, :] = x
    xc = cb_ref[...] + cw_ref[conv_w - 1:conv_w, :] * x
    for j in range(conv_w - 1):
        xc = xc + cw_ref[j:j + 1, :] * xpad[pl.ds(SUBLANE - (conv_w - 1) + j, tt), :]
    xpad[0:SUBLANE, :] = x[tt - SUBLANE:tt, :]

    xcb = xc.astype(BF16)
    r = jax.nn.sigmoid(_dot(xcb, wra_ref[...]) + bra_ref[...])
    ig = jax.nn.sigmoid(_dot(xcb, wix_ref[...]) + bix_ref[...])
    log_a = (-RG_C * _softplus(-lam_ref[...])) * r
    a = jnp.exp(log_a)
    a_scr[...] = a
    u_scr[...] = jnp.sqrt(-jnp.tanh(log_a) * (a * a + 1.0)) * (ig * xc)

    row = lax.broadcasted_iota(I32, (SUBLANE, xpad.shape[1]), 0)

    def block(i, hprev):
        r0 = pl.multiple_of(i * SUBLANE, SUBLANE)
        a = a_scr[pl.ds(r0, SUBLANE), :]
        u = u_scr[pl.ds(r0, SUBLANE), :]
        for s in (1, 2, 4):
            keep = row >= s
            u = jnp.where(keep, a * pltpu.roll(u, s, axis=0) + u, u)
            a = jnp.where(keep, a * pltpu.roll(a, s, axis=0), a)
        h = a * hprev + u
        u_scr[pl.ds(r0, SUBLANE), :] = h
        return jnp.broadcast_to(h[SUBLANE - 1:SUBLANE, :], h.shape)

    hcar[...] = lax.fori_loop(0, tt // SUBLANE, block, hcar[...])
    y_ref[...] = (u_scr[...] * gate_ref[...].astype(F32)).astype(y_ref.dtype)


def _rg_core(proj, conv_w, conv_b, w_ra, b_ra, w_ix, b_ix, lam, *, batch, seq, tt):
    nh, hd = w_ra.shape[0], w_ra.shape[1]
    cw = conv_w.shape[0]
    tt = min(tt, seq)
    assert seq % tt == 0 and tt % SUBLANE == 0 and cw - 1 <= SUBLANE
    proj3 = proj.reshape(batch, seq, 2 * nh * hd)
    vec = lambda: pl.BlockSpec((1, hd), lambda b, h, t: (0, h))
    y = pl.pallas_call(
        functools.partial(_rg_body, tt=tt, conv_w=cw),
        grid=(batch, nh, seq // tt),
        in_specs=[pl.BlockSpec((None, tt, hd), lambda b, h, t: (b, t, h)),
                  pl.BlockSpec((None, tt, hd), lambda b, h, t: (b, t, nh + h)),
                  pl.BlockSpec((cw, hd), lambda b, h, t: (0, h)),
                  vec(),
                  pl.BlockSpec((None, hd, hd), lambda b, h, t: (h, 0, 0)),
                  vec(),
                  pl.BlockSpec((None, hd, hd), lambda b, h, t: (h, 0, 0)),
                  vec(),
                  vec()],
        out_specs=pl.BlockSpec((None, tt, hd), lambda b, h, t: (b, t, h)),
        out_shape=jax.ShapeDtypeStruct((batch, seq, nh * hd), BF16),
        scratch_shapes=[pltpu.VMEM((tt + SUBLANE, hd), F32),
                        pltpu.VMEM((tt, hd), F32),
                        pltpu.VMEM((tt, hd), F32),
                        pltpu.VMEM((SUBLANE, hd), F32)],
        compiler_params=_params("arbitrary", "arbitrary", "arbitrary"),
        name="rg_core",
    )(proj3, proj3, conv_w, conv_b, w_ra, b_ra, w_ix, b_ix, lam)
    return y.reshape(batch * seq, nh * hd)


def _hg_tables(chunk, band, width):
    t = np.arange(chunk)[:, None]
    s = np.arange(chunk)[None, :]
    masks, signs = [], []
    half = chunk // 2
    while half >= band:
        same_parent = (t // (2 * half)) == (s // (2 * half))
        split = ((t // half) % 2 == 1) & ((s // half) % 2 == 0)
        masks.append((same_parent & split & (t - s >= band)).astype(np.float32))
        signs.append(np.broadcast_to(np.where((t // half) % 2 == 1, 1.0, -1.0), (chunk, width)).astype(np.float32))
        half //= 2
    return np.stack(masks), np.stack(signs)


def _hg_body(q_ref, v_ref, g_ref, lf_ref, k_ref, nw_ref, tri_ref, mask_ref, sign_ref, o_ref,
             st_ref, kpad, bpad, vpad, *, tt, chunk, band, heads):
    kd = LANE

    @pl.when(pl.program_id(2) == 0)
    def _():
        st_ref[...] = jnp.zeros_like(st_ref)

    zpad = jnp.zeros((SUBLANE, heads * kd), F32)
    kpad[0:SUBLANE, :] = zpad
    bpad[0:SUBLANE, :] = zpad
    vpad[0:SUBLANE, :] = zpad
    tri = tri_ref[...]
    row = lax.broadcasted_iota(I32, (chunk, 1), 0)

    def one_chunk(c, carry):
        r0 = pl.multiple_of(c * chunk, chunk)
        q2 = q_ref[pl.ds(r0, chunk), :].astype(F32)
        k2 = k_ref[pl.ds(r0, chunk), :].astype(F32)
        vb2 = v_ref[pl.ds(r0, chunk), :]
        g2 = g_ref[pl.ds(r0, chunk), :].astype(F32)
        lf_hi, lf_lo = _split_bf16(lf_ref[pl.ds(r0, chunk), :])
        b2 = _dot(tri, lf_hi) + _dot(tri, lf_lo)
        kpad[SUBLANE:SUBLANE + chunk, :] = k2
        bpad[SUBLANE:SUBLANE + chunk, :] = b2
        vpad[SUBLANE:SUBLANE + chunk, :] = vb2.astype(F32)

        outs = []
        for h in range(heads):
            sl = slice(h * kd, (h + 1) * kd)
            q, k, vb, b = q2[:, sl], k2[:, sl], vb2[:, sl], b2[:, sl]
            b_last = b[chunk - 1:chunk, :]

            st = st_ref[h]
            o = _dot_nt((q * jnp.exp2(b)).astype(BF16), st.astype(BF16))
            k_dec = (k * jnp.exp2(b_last - b)).astype(BF16)
            st_ref[h] = st * jnp.exp2(b_last) + _dot_tn(vb, k_dec)

            scores = jnp.zeros((chunk, chunk), BF16)
            half, level = chunk // 2, 0
            while half >= band:
                if half >= SUBLANE:
                    rows = []
                    for p0 in range(0, chunk, 2 * half):
                        r = b[p0 + half - 1:p0 + half, :]
                        lo_, mid_, hi_ = p0, p0 + half, p0 + 2 * half
                        rows.append(k[lo_:mid_] * jnp.exp2(r - b[lo_:mid_]))
                        rows.append(q[mid_:hi_] * jnp.exp2(b[mid_:hi_] - r))
                    both = jnp.concatenate(rows, axis=0).astype(BF16)
                else:
                    b3 = b.reshape(chunk // (2 * half), 2 * half, kd)
                    ref = jnp.broadcast_to(b3[:, half - 1:half, :], b3.shape).reshape(chunk, kd)
                    sign = sign_ref[level]
                    e = jnp.exp2((b - ref) * sign)
                    both = (jnp.where(sign > 0.0, q, k) * e).astype(BF16)
                scores = scores + mask_ref[level] * _dot_nt(both, both).astype(BF16)
                half //= 2
                level += 1
            o = o + _dot(scores, vb)

            o = o + jnp.sum(q * k, axis=-1, keepdims=True) * vb.astype(F32)
            for d in range(1, band):
                kd_ = kpad[pl.ds(SUBLANE - d, chunk), sl]
                bd_ = bpad[pl.ds(SUBLANE - d, chunk), sl]
                vd_ = vpad[pl.ds(SUBLANE - d, chunk), sl]
                w = q * kd_ * jnp.exp2(jnp.minimum(b - bd_, 0.0))
                sc = jnp.sum(w, axis=-1, keepdims=True)
                o = o + jnp.where(row >= d, sc, 0.0) * vd_

            ms = jnp.mean(o * o, axis=-1, keepdims=True)
            outs.append(o * lax.rsqrt(ms + RMS_EPS) * nw_ref[...] * g2[:, sl])
        y = outs[0] if heads == 1 else jnp.concatenate(outs, axis=1)
        o_ref[pl.ds(r0, chunk), :] = y.astype(o_ref.dtype)
        return carry

    lax.fori_loop(0, tt // chunk, one_chunk, 0, unroll=min(8, tt // chunk))


def _hg_core(qvg, log_f, key, norm_w, *, batch, seq, heads, tt):
    kd = norm_w.shape[-1]
    assert kd == LANE, "head key/value width must equal the lane width"
    hps = HG_HEADS_PER_STEP if heads % HG_HEADS_PER_STEP == 0 else 1
    tt = min(tt, seq)
    chunk = min(HG_CHUNK, tt)
    assert seq % tt == 0 and tt % chunk == 0 and chunk % (2 * SUBLANE) == 0 and chunk & (chunk - 1) == 0
    masks_np, signs_np = _hg_tables(chunk, HG_BAND, kd)
    masks = jnp.asarray(masks_np, dtype=BF16)
    signs = jnp.asarray(signs_np)
    tri = jnp.asarray(np.tril(np.ones((chunk, chunk), np.float32)), dtype=BF16)
    wd = hps * kd
    groups = heads // hps
    qvg3 = qvg.reshape(batch, seq, 3 * heads * kd)
    lf3 = log_f.reshape(batch, seq, heads * kd)
    k3 = key.reshape(batch, seq, heads * kd)
    blk = lambda off: pl.BlockSpec((None, tt, wd), lambda b, h, t, off=off: (b, t, off + h))
    const = lambda a: pl.BlockSpec(a.shape, lambda b, h, t: (0,) * a.ndim)
    out = pl.pallas_call(
        functools.partial(_hg_body, tt=tt, chunk=chunk, band=HG_BAND, heads=hps),
        grid=(batch, groups, seq // tt),
        in_specs=[blk(0), blk(groups), blk(2 * groups), blk(0), blk(0),
                  const(norm_w), const(tri), const(masks), const(signs)],
        out_specs=blk(0),
        out_shape=jax.ShapeDtypeStruct((batch, seq, heads * kd), BF16),
        scratch_shapes=[pltpu.VMEM((hps, kd, kd), F32),
                        pltpu.VMEM((chunk + SUBLANE, wd), F32),
                        pltpu.VMEM((chunk + SUBLANE, wd), F32),
                        pltpu.VMEM((chunk + SUBLANE, wd), F32)],
        compiler_params=_params("arbitrary", "arbitrary", "arbitrary"),
        name="hg_core",
    )(qvg3, qvg3, qvg3, lf3, k3, norm_w, tri, masks, signs)
    return out.reshape(batch * seq, heads * kd)


def _layer_norm(x, g, b):
    mu = jnp.mean(x, axis=-1, keepdims=True)
    xc = x - mu
    var = jnp.mean(xc * xc, axis=-1, keepdims=True)
    return xc * lax.rsqrt(var + LN_EPS) * g + b


def _first_max(vals):
    best, idx = vals[0], jnp.zeros(vals[0].shape, I32)
    for i in range(1, len(vals)):
        better = vals[i] > best
        best = jnp.where(better, vals[i], best)
        idx = jnp.where(better, i, idx)
    return best, idx


def _top2(vals):
    top1, i1 = _first_max(vals)
    rest = [jnp.where(i1 == i, -jnp.inf, vals[i]) for i in range(len(vals))]
    top2, i2 = _first_max(rest)
    return top1, i1, top2, i2


def _route(logit_rows, n_groups):
    n_exp = len(logit_rows)
    per = n_exp // n_groups
    mx = functools.reduce(jnp.maximum, logit_rows)
    ex = [jnp.exp(l - mx) for l in logit_rows]
    den = functools.reduce(lambda a, b: a + b, ex)
    p = [e / den for e in ex]
    scores = []
    for g in range(n_groups):
        top1, _, top2, _ = _top2(p[g * per:(g + 1) * per])
        scores.append(top1 + top2)
    _, g_sel = _first_max(scores)
    in_group = []
    for i in range(per):
        val = p[i]
        for g in range(1, n_groups):
            val = jnp.where(g_sel == g, p[g * per + i], val)
        in_group.append(val)
    p1, i1, p2, i2 = _top2(in_group)
    tot = p1 + p2
    return g_sel * per + i1, g_sel * per + i2, p1 / tot, p2 / tot


def _ln_route_body(mix_ref, h_ref, g_ref, b_ref, rwh_ref, rwl_ref, rb_ref, o_ref, eid_ref, wt_ref,
                   *, alpha, n_exp, n_groups):
    y = _layer_norm(alpha * h_ref[...].astype(F32) + mix_ref[...], g_ref[...], b_ref[...])
    o_ref[...] = _pack_pairs(y)
    y_hi, y_lo = _split_bf16(y)
    logits = _dot(y_hi, rwh_ref[...]) + (_dot(y_hi, rwl_ref[...]) + _dot(y_lo, rwh_ref[...]))
    lt = logits.T + rb_ref[...]
    e1, e2, w1, w2 = _route([lt[e:e + 1, :] for e in range(n_exp)], n_groups)
    eid_ref[0:1, :] = e1
    eid_ref[1:2, :] = e2
    wt_ref[0:1, :] = w1
    wt_ref[1:2, :] = w2


def _ln_route(mix, h, g, b, router_w, router_b, *, alpha, tr):
    n, d = mix.shape
    n_exp = router_w.shape[1]
    tr = min(tr, n)
    assert n % tr == 0 and n_exp <= LANE and n_exp % N_GROUPS == 0 and d % (2 * LANE) == 0
    rw = jnp.zeros((d, LANE), F32).at[:, :n_exp].set(router_w.astype(F32))
    rw_hi = rw.astype(BF16)
    rw_lo = (rw - rw_hi.astype(F32)).astype(BF16)
    rb = jnp.zeros((LANE, 1), F32).at[:n_exp, 0].set(router_b.astype(F32))
    row = lambda: pl.BlockSpec((tr, d), lambda i: (i, 0))
    full = lambda shape: pl.BlockSpec(shape, lambda i: (0, 0))
    return pl.pallas_call(
        functools.partial(_ln_route_body, alpha=alpha, n_exp=n_exp, n_groups=N_GROUPS),
        grid=(n // tr,),
        in_specs=[row(), row(), full((1, d)), full((1, d)), full((d, LANE)), full((d, LANE)),
                  full((LANE, 1))],
        out_specs=[pl.BlockSpec((tr, d // 2), lambda i: (i, 0)),
                   pl.BlockSpec((2, tr), lambda i: (0, i)), pl.BlockSpec((2, tr), lambda i: (0, i))],
        out_shape=[jax.ShapeDtypeStruct((n, d // 2), U32), jax.ShapeDtypeStruct((2, n), I32),
                   jax.ShapeDtypeStruct((2, n), F32)],
        compiler_params=_params("arbitrary"),
        name="ln_route",
    )(mix, h, g, b, rw_hi, rw_lo, rb)


def _start_row_gather(src_hbm, idx_ref, dst, sem, first_row, n_rows):
    for r in range(first_row, first_row + n_rows):
        pltpu.make_async_copy(src_hbm.at[pl.ds(idx_ref[0, 0, r], 1)], dst.at[pl.ds(r, 1)], sem).start()


def _wait_row_gather(src_hbm, dst, sem, n_rows):
    for r in range(n_rows):
        pltpu.make_async_copy(src_hbm.at[pl.ds(0, 1)], dst.at[pl.ds(r, 1)], sem).wait()


def _pipelined_gather(src_hbm, idx_ref, idx_next_ref, buf, sem, n_rows):
    i = pl.program_id(0)
    slot = lax.rem(i, 2)

    @pl.when(i == 0)
    def _():
        _start_row_gather(src_hbm, idx_ref, buf.at[0], sem.at[0], 0, n_rows)

    @pl.when(i + 1 < pl.num_programs(0))
    def _():
        _start_row_gather(src_hbm, idx_next_ref, buf.at[1 - slot], sem.at[1 - slot], 0, n_rows)

    _wait_row_gather(src_hbm, buf.at[slot], sem.at[slot], n_rows)
    return slot


def _idx_specs(tiles, width):
    return [pl.BlockSpec((1, 1, width), lambda i, *_: (i, 0, 0), memory_space=pltpu.SMEM),
            pl.BlockSpec((1, 1, width), lambda i, *_: (jnp.minimum(i + 1, tiles - 1), 0, 0),
                         memory_space=pltpu.SMEM)]


def _load_expert_weights(w_hbm, layer, expert, w_bf, stage, wsem, *, cols):
    ff = w_bf.shape[2]
    n_chunks = (2 * ff) // cols
    slots = stage.shape[0]

    def chunk_copy(c):
        return pltpu.make_async_copy(w_hbm.at[layer, expert, :, pl.ds(c * cols, cols)],
                                     stage.at[c % slots], wsem.at[c % slots])

    for c in range(min(slots - 1, n_chunks)):
        chunk_copy(c).start()
    for c in range(n_chunks):
        if c + slots - 1 < n_chunks:
            chunk_copy(c + slots - 1).start()
        chunk_copy(c).wait()
        half, col = divmod(c * cols, ff)
        w_bf[half, :, col:col + cols] = stage[c % slots].astype(BF16)


def _moe_up_body(te_ref, nu_ref, src_ref, src_next_ref, h_hbm, w_hbm, o_ref, buf, sem, w_bf, stage, wsem, x_bf,
                 *, tm, layer, cols, col_blocks):
    i = pl.program_id(0)
    n_used = nu_ref[0]
    last = pl.num_programs(0) - 1
    slot = lax.rem(i, 2)
    expert = te_ref[i]
    ff = o_ref.shape[1]
    cb = ff // col_blocks
    row_split = [(c * tm) // col_blocks for c in range(col_blocks + 1)]

    @pl.when(i == 0)
    def _():
        _start_row_gather(h_hbm, src_ref, buf.at[0], sem.at[0], 0, tm)

    @pl.when((i < n_used) & ((i == 0) | (expert != te_ref[jnp.maximum(i - 1, 0)])))
    def _():
        _load_expert_weights(w_hbm, layer, expert, w_bf, stage, wsem, cols=cols)

    @pl.when(i < n_used)
    def _():
        _wait_row_gather(h_hbm, buf.at[slot], sem.at[slot], tm)
        x_bf[...] = _unpack_pairs(buf[slot]).astype(BF16)
        for c in range(col_blocks):
            _start_row_gather(h_hbm, src_next_ref, buf.at[1 - slot], sem.at[1 - slot], row_split[c],
                              row_split[c + 1] - row_split[c])
            x = x_bf[...]
            gate = _dot(x, w_bf[0, :, c * cb:(c + 1) * cb])
            up = _dot(x, w_bf[1, :, c * cb:(c + 1) * cb])
            o_ref[:, c * cb:(c + 1) * cb] = (_silu(gate) * up).astype(o_ref.dtype)

    @pl.when(i >= n_used)
    def _():
        o_ref[...] = jnp.zeros_like(o_ref)

    @pl.when(i == n_used)
    def _():
        _wait_row_gather(h_hbm, buf.at[slot], sem.at[slot], tm)

    @pl.when((i == last) & (n_used > last))
    def _():
        _wait_row_gather(h_hbm, buf.at[1 - slot], sem.at[1 - slot], tm)


def _moe_up(h_packed, src_token, w_in, layer, tile_expert, n_used, *, tm):
    n, dh = h_packed.shape
    d = 2 * dh
    p = src_token.shape[0]
    ff = w_in.shape[3] // 2
    cols = min(LANE, ff)
    col_blocks = ff // MXU_WIDTH if ff % MXU_WIDTH == 0 else 1
    assert p % tm == 0 and ff % cols == 0
    tiles = p // tm
    src_tiles = src_token.reshape(tiles, 1, tm)
    return pl.pallas_call(
        functools.partial(_moe_up_body, tm=tm, layer=layer, cols=cols, col_blocks=col_blocks),
        grid_spec=pltpu.PrefetchScalarGridSpec(
            num_scalar_prefetch=2,
            grid=(tiles,),
            in_specs=_idx_specs(tiles, tm) + [pl.BlockSpec(memory_space=pl.ANY),
                                              pl.BlockSpec(memory_space=pl.ANY)],
            out_specs=pl.BlockSpec((tm, ff), lambda i, te, nu: (i, 0)),
            scratch_shapes=[pltpu.VMEM((2, tm, dh), U32), pltpu.SemaphoreType.DMA((2,)),
                            pltpu.VMEM((2, d, ff), BF16), pltpu.VMEM((WEIGHT_STAGE_SLOTS, d, cols), w_in.dtype),
                            pltpu.SemaphoreType.DMA((WEIGHT_STAGE_SLOTS,)), pltpu.VMEM((tm, d), BF16)]),
        out_shape=jax.ShapeDtypeStruct((p, ff), BF16),
        compiler_params=_params("arbitrary"),
        name="moe_up",
    )(tile_expert, n_used, src_tiles, src_tiles, h_packed, w_in)


def _moe_down_body(te_ref, nu_ref, x_ref, w_ref, o_ref):
    @pl.when(pl.program_id(0) < nu_ref[0])
    def _():
        o_ref[...] = _pack_pairs(_dot(x_ref[...], w_ref[...]))

    @pl.when(pl.program_id(0) >= nu_ref[0])
    def _():
        o_ref[...] = jnp.zeros_like(o_ref)


def _moe_down(h1, w_out, layer, tile_expert, n_used, *, tm):
    p, ff = h1.shape
    d = w_out.shape[3]
    assert p % tm == 0
    return pl.pallas_call(
        _moe_down_body,
        grid_spec=pltpu.PrefetchScalarGridSpec(
            num_scalar_prefetch=2,
            grid=(p // tm,),
            in_specs=[pl.BlockSpec((tm, ff), lambda i, te, nu: (i, 0)),
                      pl.BlockSpec((None, None, ff, d), lambda i, te, nu: (layer, te[i], 0, 0))],
            out_specs=pl.BlockSpec((tm, d // 2), lambda i, te, nu: (i, 0))),
        out_shape=jax.ShapeDtypeStruct((p, d // 2), U32),
        compiler_params=_params("arbitrary"),
        name="moe_down",
    )(tile_expert, n_used, h1, w_out)


def _combine_ln_body(pos_ref, pos_next_ref, ys_hbm, h_ref, wt_ref, g_ref, b_ref, o_ref, buf, sem,
                     *, tc, alpha):
    slot = _pipelined_gather(ys_hbm, pos_ref, pos_next_ref, buf, sem, 2 * tc)
    wt = wt_ref[...]
    rows = buf[slot]
    ffn = wt[:, 0:1] * _unpack_pairs(rows[0:tc]) + wt[:, 1:2] * _unpack_pairs(rows[tc:2 * tc])
    x = alpha * _unpack_pairs(h_ref[...]) + ffn
    o_ref[...] = _layer_norm(x, g_ref[...], b_ref[...]).astype(o_ref.dtype)


def _combine_ln(ys, pos, h_packed, wts, g, b, *, alpha, out_dtype, tc):
    n, dh = h_packed.shape
    d = 2 * dh
    tc = min(tc, n)
    assert n % tc == 0
    pos_tiles = jnp.concatenate([pos[0].reshape(n // tc, 1, tc), pos[1].reshape(n // tc, 1, tc)], axis=2)
    return pl.pallas_call(
        functools.partial(_combine_ln_body, tc=tc, alpha=alpha),
        grid=(n // tc,),
        in_specs=_idx_specs(n // tc, 2 * tc) + [
            pl.BlockSpec(memory_space=pl.ANY),
            pl.BlockSpec((tc, dh), lambda i: (i, 0)),
            pl.BlockSpec((tc, 2), lambda i: (i, 0)),
            pl.BlockSpec((1, d), lambda i: (0, 0)),
            pl.BlockSpec((1, d), lambda i: (0, 0))],
        out_specs=pl.BlockSpec((tc, d), lambda i: (i, 0)),
        out_shape=jax.ShapeDtypeStruct((n, d), out_dtype),
        scratch_shapes=[pltpu.VMEM((2, 2 * tc, dh), U32), pltpu.SemaphoreType.DMA((2,))],
        compiler_params=_params("arbitrary"),
        name="moe_combine_ln",
    )(pos_tiles, pos_tiles, ys, h_packed, wts.T, g, b)


def _moe_plan(eid, n_exp, tm):
    n = eid.shape[1]
    flat = eid.reshape(-1)
    onehot = (flat[:, None] == jnp.arange(n_exp, dtype=I32)[None, :]).astype(I32)
    csum = jnp.cumsum(onehot, axis=0)
    cnt = csum[-1]
    tiles = (cnt + tm - 1) // tm
    tile_end = jnp.cumsum(tiles)
    tile_start = tile_end - tiles
    grp_start = jnp.cumsum(cnt) - cnt
    pos = jnp.sum(onehot * (csum - 1 + (tile_start * tm)[None, :]), axis=1).reshape(2, n)

    n_tiles = (2 * n) // tm + n_exp
    n_used = tile_end[-1]
    tile_id = jnp.arange(n_tiles, dtype=I32)
    te = jnp.minimum(jnp.searchsorted(tile_end, tile_id, side="right"), n_exp - 1).astype(I32)
    te = jnp.where(tile_id < n_used, te, te[jnp.maximum(n_used - 1, 0)])

    order = jnp.argsort(flat, stable=True).astype(I32)
    base = grp_start[te] + (tile_id - tile_start[te]) * tm
    last = grp_start[te] + cnt[te] - 1
    idx = jnp.minimum(base[:, None] + jnp.arange(tm, dtype=I32)[None, :], last[:, None])
    src = order[jnp.clip(idx, 0, 2 * n - 1).reshape(-1)] % n
    return pos.astype(I32), src.astype(I32), te, n_used.astype(I32).reshape(1)


def _moe_ln(h_packed, eid, wts, w_in, w_out, layer, g, b, *, alpha, out_dtype, tm):
    n_exp = w_in.shape[1]
    pos, src, tile_expert, n_used = _moe_plan(eid, n_exp, tm)
    h1 = _moe_up(h_packed, src, w_in, layer, tile_expert, n_used, tm=tm)
    ys = _moe_down(h1, w_out, layer, tile_expert, n_used, tm=tm)
    return _combine_ln(ys, pos, h_packed, wts, g, b, alpha=alpha, out_dtype=out_dtype, tc=256)


def _pad_heads(a, nh, hd, hdp, axis):
    shape = a.shape
    a = a.reshape(shape[:axis] + (nh, hd) + shape[axis + 1:])
    pad = [(0, 0)] * a.ndim
    pad[axis + 1] = (0, hdp - hd)
    a = jnp.pad(a, pad)
    return a.reshape(shape[:axis] + (nh * hdp,) + shape[axis + 1:])


def kernel(x, rg_w_in, rg_conv_w, rg_conv_b, rg_w_ra, rg_b_ra, rg_w_ix, rg_b_ix, rg_lambda, rg_w_out,
           hg_w_in, hg_norm_w, hg_w_out, hg_lb_logits, router_w, router_b, moe_w_in, moe_w_out, ln_g, ln_b):
    batch, seq, d = x.shape
    n = batch * seq
    depth = ln_g.shape[0]
    alpha = (2.0 * depth) ** 0.25
    nh, hd = rg_w_ra.shape[1], rg_w_ra.shape[2]
    hdp = _round_up(hd, LANE)
    hv = hg_w_out.shape[1]
    hk = (hg_w_in.shape[2] - 2 * hv) // 2
    heads = hv // hg_norm_w.shape[1]
    assert hk == hv, "HGRN2 key and value widths must match"
    moe_tm = min(512, n)
    moe_w_out_bf = moe_w_out.astype(BF16)

    h = x.reshape(n, d)
    h_bf = h.astype(BF16)
    for layer in range(depth):
        j = layer // 2
        if layer % 2 == 0:
            w_in = _pad_heads(rg_w_in[j].astype(BF16).reshape(d, 2, nh * hd), nh, hd, hdp, 2)
            proj = _matmul(h_bf, w_in.reshape(d, 2 * nh * hdp), acts=(jax.nn.gelu, _identity),
                           col_starts=(0, nh * hdp), width=nh * hdp, out_dtype=BF16,
                           tm=1024, tn=1024, name="rg_in_proj")
            pad_vec = lambda v: _pad_heads(v.reshape(1, nh * hd).astype(F32), nh, hd, hdp, 1)
            pad_mat = lambda w: jnp.pad(w, ((0, 0), (0, hdp - hd), (0, hdp - hd))).astype(BF16)
            y = _rg_core(proj, _pad_heads(rg_conv_w[j].astype(F32), nh, hd, hdp, 1), pad_vec(rg_conv_b[j]),
                         pad_mat(rg_w_ra[j]), pad_vec(rg_b_ra[j]), pad_mat(rg_w_ix[j]), pad_vec(rg_b_ix[j]),
                         pad_vec(rg_lambda[j]), batch=batch, seq=seq, tt=1024)
            w_out = _pad_heads(rg_w_out[j].astype(BF16), nh, hd, hdp, 0)
            mix = _matmul(y, w_out, acts=(_identity,), col_starts=(0,), width=d, out_dtype=F32,
                          tm=512, tn=1024, name="rg_out_proj")
        else:
            w = hg_w_in[j].astype(BF16)
            qvg = _matmul(h_bf, w, acts=(_silu, _identity, _silu), col_starts=(0, 2 * hk, 2 * hk + hv),
                          width=hk, out_dtype=BF16, tm=1024, tn=1024, name="hg_in_proj")
            log_f, key = _hg_gate_proj(h_bf, w, hg_lb_logits.astype(F32), col_start=hk, width=hk,
                                       layer=layer, tm=1024, tn=1024)
            o = _hg_core(qvg, log_f, key, hg_norm_w[j].reshape(1, -1).astype(F32),
                         batch=batch, seq=seq, heads=heads, tt=2048)
            mix = _matmul(o, hg_w_out[j].astype(BF16), acts=(_identity,), col_starts=(0,), width=d,
                          out_dtype=F32, tm=1024, tn=1024, name="hg_out_proj")
        g1, b1 = ln_g[layer, 0].reshape(1, d).astype(F32), ln_b[layer, 0].reshape(1, d).astype(F32)
        g2, b2 = ln_g[layer, 1].reshape(1, d).astype(F32), ln_b[layer, 1].reshape(1, d).astype(F32)
        h_packed, eid, wts = _ln_route(mix, h, g1, b1, router_w, router_b, alpha=alpha, tr=256)
        last = layer == depth - 1
        h = _moe_ln(h_packed, eid, wts, moe_w_in, moe_w_out_bf, layer, g2, b2, alpha=alpha,
                    out_dtype=x.dtype if last else BF16, tm=moe_tm)
        h_bf = h
    return h.reshape(batch, seq, d)
```
